```python
import math
import jax
import jax.numpy as jnp
from jax import lax
import numpy as np

D_MODEL = 1024
BATCH = 2
SEQ = 8192
DEPTH = 2

GRID_W = 64
CTX_LEN = 256
W_BRANCH = 256
N_BRANCH = 4
S5_GROUP = 16
S5_GROUPS = W_BRANCH // S5_GROUP
S5_STATE = 64
FNET_GROUPS = 4
POOL_WINDOWS = (2, 4, 8, 16)
CONV_WIDTH = 31
D_FF = 4 * D_MODEL
D_IN = 5 * W_BRANCH + N_BRANCH * D_MODEL
N_MOD = 6
EPS = 1e-6
POS_BASE = 10000.0

kernel_name = 'hybrid_s5_fnet_pool_conformer_dit'


def _rms_norm(x, g):
    xf = x.astype(jnp.float32)
    y = xf * lax.rsqrt(jnp.mean(xf * xf, axis=-1, keepdims=True) + EPS)
    return (y * g.astype(jnp.float32)).astype(x.dtype)


def _layer_norm(x, g, b):
    xf = x.astype(jnp.float32)
    mu = jnp.mean(xf, axis=-1, keepdims=True)
    xc = xf - mu
    y = xc * lax.rsqrt(jnp.mean(xc * xc, axis=-1, keepdims=True) + EPS)
    return (y * g.astype(jnp.float32) + b.astype(jnp.float32)).astype(x.dtype)


def _modulate(x, g, shift, scale):
    return _rms_norm(x, g) * (1 + scale) + shift


def _pos_embed_2d(rows, dtype):
    row = jnp.repeat(jnp.arange(rows), GRID_W)
    col = jnp.tile(jnp.arange(GRID_W), rows)
    quarter = D_MODEL // 4
    freq = 1.0 / (POS_BASE ** (jnp.arange(quarter, dtype=jnp.float32) / quarter))

    def enc(pos):
        ang = pos.astype(jnp.float32)[:, None] * freq[None, :]
        return jnp.concatenate([jnp.sin(ang), jnp.cos(ang)], axis=-1)

    return jnp.concatenate([enc(row), enc(col)], axis=-1).astype(dtype)


def _s5_discretise(lam_re, lam_im, log_dt, b_re, b_im):
    lam_re = lam_re.astype(jnp.float32)
    lam_im = lam_im.astype(jnp.float32)
    dt = jnp.exp(log_dt.astype(jnp.float32))[:, None]
    mag = jnp.exp(lam_re * dt)
    ang = lam_im * dt
    a_re = mag * jnp.cos(ang)
    a_im = mag * jnp.sin(ang)
    den = lam_re * lam_re + lam_im * lam_im
    f_re = ((a_re - 1) * lam_re + a_im * lam_im) / den
    f_im = (a_im * lam_re - (a_re - 1) * lam_im) / den
    b_re = b_re.astype(jnp.float32)
    b_im = b_im.astype(jnp.float32)
    bb_re = f_re[..., None] * b_re - f_im[..., None] * b_im
    bb_im = f_re[..., None] * b_im + f_im[..., None] * b_re
    return a_re, a_im, bb_re, bb_im


def _complex_affine_combine(e1, e2):
    a1r, a1i, b1r, b1i = e1
    a2r, a2i, b2r, b2i = e2
    return (a2r * a1r - a2i * a1i,
            a2r * a1i + a2i * a1r,
            a2r * b1r - a2i * b1i + b2r,
            a2r * b1i + a2i * b1r + b2i)


def _s5_states(u, lam_re, lam_im, log_dt, b_re, b_im, init):
    bsz, length, _ = u.shape
    uf = u.astype(jnp.float32).reshape(bsz, length, S5_GROUPS, S5_GROUP)
    states = []
    for k, rev in enumerate((False, True)):
        a_re, a_im, bb_re, bb_im = _s5_discretise(lam_re[k], lam_im[k], log_dt[k], b_re[k], b_im[k])
        bu_re = jnp.einsum('blgc,gpc->blgp', uf, bb_re)
        bu_im = jnp.einsum('blgc,gpc->blgp', uf, bb_im)
        if init is not None:
            h0_re, h0_im = init[k]
            first = length - 1 if rev else 0
            bu_re = bu_re.at[:, first].add(a_re * h0_re - a_im * h0_im)
            bu_im = bu_im.at[:, first].add(a_re * h0_im + a_im * h0_re)
        elems = (jnp.broadcast_to(a_re, bu_re.shape), jnp.broadcast_to(a_im, bu_im.shape), bu_re, bu_im)
        _, _, h_re, h_im = lax.associative_scan(_complex_affine_combine, elems, reverse=rev, axis=1)
        states.append((h_re, h_im))
    return states


def _s5_final(states):
    (f_re, f_im), (r_re, r_im) = states
    return [(f_re[:, -1], f_im[:, -1]), (r_re[:, 0], r_im[:, 0])]


def _s5_readout(u, states, c_re, c_im, d, w_glu):
    bsz, length, _ = u.shape
    y = d.astype(jnp.float32) * u.astype(jnp.float32)
    for k in range(2):
        h_re, h_im = states[k]
        y_k = (jnp.einsum('blgp,gcp->blgc', h_re, c_re[k].astype(jnp.float32))
               - jnp.einsum('blgp,gcp->blgc', h_im, c_im[k].astype(jnp.float32)))
        y = y + y_k.reshape(bsz, length, W_BRANCH)
    y = jax.nn.gelu(y).astype(u.dtype)
    return y * jax.nn.sigmoid(y @ w_glu)


def _fnet(u, w):
    bsz, length, width = u.shape
    uf = u.astype(jnp.float32).reshape(bsz, length, FNET_GROUPS, width // FNET_GROUPS)
    y = jnp.fft.fft2(uf, axes=(1, 3), norm='ortho').real.reshape(bsz, length, width)
    return y.astype(u.dtype) @ w


def _multiscale_pool(u, w, scale):
    bsz, length, width = u.shape
    cw = width // len(POOL_WINDOWS)
    uf = u.astype(jnp.float32)
    cs = jnp.concatenate([jnp.zeros((bsz, 1, width), jnp.float32), jnp.cumsum(uf, axis=1)], axis=1)
    t = jnp.arange(length)
    outs = []
    for gi, win in enumerate(POOL_WINDOWS):
        lo = jnp.clip(t - win // 2, 0, length)
        hi = jnp.clip(t + win // 2, 0, length)
        csg = cs[..., gi * cw:(gi + 1) * cw]
        s = jnp.take(csg, hi, axis=1) - jnp.take(csg, lo, axis=1)
        cnt = (hi - lo).astype(jnp.float32)[None, :, None]
        outs.append(s / cnt - uf[..., gi * cw:(gi + 1) * cw])
    y = jnp.stack(outs, axis=2)
    y = jnp.einsum('blgc,gcd->blgd', y, w.astype(jnp.float32)).reshape(bsz, length, width)
    return (y * scale.astype(jnp.float32)).astype(u.dtype)


def _conformer_conv(u_val, u_gate, w_dw, b_dw, ln_g, ln_b, w_pw):
    v = u_val * jax.nn.sigmoid(u_gate)
    y = lax.conv_general_dilated(
        v, w_dw.astype(v.dtype)[:, None, :], window_strides=(1,),
        padding=[(CONV_WIDTH // 2, CONV_WIDTH // 2)],
        dimension_numbers=('NWC', 'WIO', 'NWC'), feature_group_count=v.shape[-1]) + b_dw
    y = jax.nn.silu(_layer_norm(y, ln_g, ln_b))
    return y @ w_pw


def _token_mixer(z, s5_states, p):
    bsz, length, _ = z.shape
    wb = W_BRANCH
    y_s5 = _s5_readout(z[..., 0:wb], s5_states, p['s5_c_re'], p['s5_c_im'], p['s5_d'], p['s5_w_glu'])
    y_fn = _fnet(z[..., wb:2 * wb], p['fnet_w'])
    y_pl = _multiscale_pool(z[..., 2 * wb:3 * wb], p['pool_w'], p['pool_scale'])
    y_cv = _conformer_conv(z[..., 3 * wb:4 * wb], z[..., 4 * wb:5 * wb], p['conv_w'], p['conv_b'],
                           p['conv_ln_g'], p['conv_ln_b'], p['conv_w_out'])
    gates = jax.nn.sigmoid(z[..., 5 * wb:].reshape(bsz, length, N_BRANCH, D_MODEL))
    branches = jnp.stack([y_s5, y_fn, y_pl, y_cv], axis=2)
    proj = jnp.einsum('blkc,kcd->blkd', branches, p['w_branch'])
    merged = jnp.einsum('blkd,blkd->bld', gates, proj)
    return merged @ p['w_out']


def _sq_relu_mlp(h, w1, w2):
    a = jax.nn.relu(h @ w1)
    return (a * a) @ w2


def setup_inputs(seed: int = 0) -> dict:
    key = jax.random.key(seed)
    ks = iter(jax.random.split(key, 40))

    def nrm(shape, scale):
        return jax.random.normal(next(ks), shape, jnp.float32) * scale

    n = jnp.arange(S5_STATE, dtype=jnp.float32)
    s5_shape = (DEPTH, 2, S5_GROUPS, S5_STATE)
    return {
        'x': nrm((BATCH, SEQ, D_MODEL), 1.0),
        'c': nrm((BATCH, D_MODEL), 1.0),
        'ctx': nrm((BATCH, CTX_LEN, D_MODEL), 1.0),
        'c_ctx': nrm((D_MODEL,), 1.0),
        'w_mod': nrm((DEPTH, D_MODEL, N_MOD * D_MODEL), D_MODEL ** -0.5),
        'b_mod': nrm((DEPTH, N_MOD * D_MODEL), 0.02),
        'g_norm1': 1.0 + nrm((DEPTH, D_MODEL), 0.02),
        'w_in': nrm((DEPTH, D_MODEL, D_IN), D_MODEL ** -0.5),
        's5_lam_re': -0.5 + nrm(s5_shape, 0.01),
        's5_lam_im': math.pi * n + nrm(s5_shape, 0.01),
        's5_log_dt': jax.random.uniform(next(ks), (DEPTH, 2, S5_GROUPS), jnp.float32,
                                        math.log(1e-3), math.log(1e-1)),
        's5_b_re': nrm((DEPTH, 2, S5_GROUPS, S5_STATE, S5_GROUP), (2 * S5_GROUP) ** -0.5),
        's5_b_im': nrm((DEPTH, 2, S5_GROUPS, S5_STATE, S5_GROUP), (2 * S5_GROUP) ** -0.5),
        's5_c_re': nrm((DEPTH, 2, S5_GROUPS, S5_GROUP, S5_STATE), S5_STATE ** -0.5),
        's5_c_im': nrm((DEPTH, 2, S5_GROUPS, S5_GROUP, S5_STATE), S5_STATE ** -0.5),
        's5_d': nrm((DEPTH, W_BRANCH), 1.0),
        's5_w_glu': nrm((DEPTH, W_BRANCH, W_BRANCH), W_BRANCH ** -0.5),
        'fnet_w': nrm((DEPTH, W_BRANCH, W_BRANCH), W_BRANCH ** -0.5),
        'pool_w': nrm((DEPTH, len(POOL_WINDOWS), W_BRANCH // 4, W_BRANCH // 4), (W_BRANCH // 4) ** -0.5),
        'pool_scale': 1.0 + nrm((DEPTH, W_BRANCH), 0.02),
        'conv_w': nrm((DEPTH, CONV_WIDTH, W_BRANCH), CONV_WIDTH ** -0.5),
        'conv_b': nrm((DEPTH, W_BRANCH), 0.02),
        'conv_ln_g': 1.0 + nrm((DEPTH, W_BRANCH), 0.02),
        'conv_ln_b': nrm((DEPTH, W_BRANCH), 0.02),
        'conv_w_out': nrm((DEPTH, W_BRANCH, W_BRANCH), W_BRANCH ** -0.5),
        'w_branch': nrm((DEPTH, N_BRANCH, W_BRANCH, D_MODEL), W_BRANCH ** -0.5),
        'w_out': nrm((DEPTH, D_MODEL, D_MODEL), D_MODEL ** -0.5),
        'g_norm2': 1.0 + nrm((DEPTH, D_MODEL), 0.02),
        'mlp_w1': nrm((DEPTH, D_MODEL, D_FF), D_MODEL ** -0.5),
        'mlp_w2': nrm((DEPTH, D_FF, D_MODEL), D_FF ** -0.5),
        'g_final': 1.0 + nrm((D_MODEL,), 0.02),
    }


def reference(x, c, ctx, c_ctx, w_mod, b_mod, g_norm1, w_in, s5_lam_re, s5_lam_im, s5_log_dt,
              s5_b_re, s5_b_im, s5_c_re, s5_c_im, s5_d, s5_w_glu, fnet_w, pool_w, pool_scale,
              conv_w, conv_b, conv_ln_g, conv_ln_b, conv_w_out, w_branch, w_out, g_norm2,
              mlp_w1, mlp_w2, g_final):
    rows = x.shape[1] // GRID_W
    x = x + _pos_embed_2d(rows, x.dtype)[None]
    xc = ctx
    c_act = jax.nn.silu(c)
    cc_act = jax.nn.silu(c_ctx)
    for l in range(DEPTH):
        p = {
            's5_c_re': s5_c_re[l], 's5_c_im': s5_c_im[l], 's5_d': s5_d[l], 's5_w_glu': s5_w_glu[l],
            'fnet_w': fnet_w[l], 'pool_w': pool_w[l], 'pool_scale': pool_scale[l],
            'conv_w': conv_w[l], 'conv_b': conv_b[l], 'conv_ln_g': conv_ln_g[l], 'conv_ln_b': conv_ln_b[l],
            'conv_w_out': conv_w_out[l], 'w_branch': w_branch[l], 'w_out': w_out[l],
        }
        s5p = (s5_lam_re[l], s5_lam_im[l], s5_log_dt[l], s5_b_re[l], s5_b_im[l])
        mod = (c_act @ w_mod[l] + b_mod[l])[:, None, :]
        sh1, sc1, ga1, sh2, sc2, ga2 = jnp.split(mod, N_MOD, axis=-1)
        mod_c = cc_act @ w_mod[l] + b_mod[l]
        csh1, csc1, cga1, csh2, csc2, cga2 = jnp.split(mod_c, N_MOD, axis=-1)

        hc = _modulate(xc, g_norm1[l], csh1, csc1)
        if l == DEPTH - 1:
            ctx_states = _s5_states(hc @ w_in[l][:, :W_BRANCH], *s5p, None)
        else:
            zc = hc @ w_in[l]
            ctx_states = _s5_states(zc[..., :W_BRANCH], *s5p, None)
            xc = xc + cga1 * _token_mixer(zc, ctx_states, p)
            xc = xc + cga2 * _sq_relu_mlp(_modulate(xc, g_norm2[l], csh2, csc2), mlp_w1[l], mlp_w2[l])
        ctx_final = _s5_final(ctx_states)

        h = _modulate(x, g_norm1[l], sh1, sc1)
        z = h @ w_in[l]
        lat_states = _s5_states(z[..., :W_BRANCH], *s5p, ctx_final)
        x = x + ga1 * _token_mixer(z, lat_states, p)
        x = x + ga2 * _sq_relu_mlp(_modulate(x, g_norm2[l], sh2, sc2), mlp_w1[l], mlp_w2[l])
    return _rms_norm(x, g_final)
```

```python
import functools
import math

import numpy as np
import jax
import jax.numpy as jnp
from jax import lax
from jax.experimental import pallas as pl
from jax.experimental.pallas import tpu as pltpu

F32 = jnp.float32
BF16 = jnp.bfloat16

D_MODEL = 1024
W_BRANCH = 256
N_BRANCH = 4
N_MOD = 6
D_FF = 4 * D_MODEL
S5_GROUPS = 16
S5_GROUP = 16
S5_STATE = 64
FNET_GROUP = 64
GRID_W = 64
CONV_WIDTH = 31
POOL_HALF = (1, 2, 4, 8)
EPS = 1e-6
POS_BASE = 10000.0

S5_CHUNK = 16
FFT_RADIX = 8
HALO = 16
LANES = 128
VMEM_LIMIT = 56 * 1024 * 1024


def _bdot(a, b):
    return jnp.dot(a, b, preferred_element_type=F32)


def _rms(x, g):
    return x * lax.rsqrt(jnp.mean(x * x, axis=-1, keepdims=True) + EPS) * g


def _mod_rows(mod_ref, row, n):
    return [mod_ref[pl.ds(row, 1), k * D_MODEL:(k + 1) * D_MODEL] for k in range(n)]


def _mod_kernel(c_ref, w_ref, b_ref, o_ref):
    c = c_ref[...]
    act = c * jax.nn.sigmoid(c)
    o_ref[...] = jnp.dot(act, w_ref[...], preferred_element_type=F32,
                         precision=lax.Precision.HIGHEST) + b_ref[...]


def _mod_call(cpad, w_mod, b_mod):
    depth, d, n = w_mod.shape
    tc = 1536
    return pl.pallas_call(
        _mod_kernel,
        grid=(depth, n // tc),
        in_specs=[
            pl.BlockSpec((8, d), lambda l, j: (0, 0)),
            pl.BlockSpec((None, d, tc), lambda l, j: (l, 0, j)),
            pl.BlockSpec((None, 1, tc), lambda l, j: (l, 0, j)),
        ],
        out_specs=pl.BlockSpec((None, 8, tc), lambda l, j: (l, 0, j)),
        out_shape=jax.ShapeDtypeStruct((depth, 8, n), F32),
        compiler_params=pltpu.CompilerParams(
            dimension_semantics=("parallel", "parallel"), vmem_limit_bytes=VMEM_LIMIT),
        name="mod",
    )(cpad, w_mod, b_mod.reshape(depth, 1, n))


def _pre_kernel(*refs, has_pos, mod_row):
    if has_pos:
        x_ref, pos_ref, mod_ref, g1_ref, w5_ref, us_ref, uf_ref, up_ref, v_ref = refs
        x = x_ref[...] + pos_ref[...]
    else:
        x_ref, mod_ref, g1_ref, w5_ref, us_ref, uf_ref, up_ref, v_ref = refs
        x = x_ref[...]
    row = pl.program_id(0) if mod_row is None else mod_row
    sh, sc = _mod_rows(mod_ref, row, 2)
    h = (_rms(x, g1_ref[...]) * (1.0 + sc) + sh).astype(BF16)
    z = _bdot(h, w5_ref[...])
    wb = W_BRANCH
    us_ref[...] = z[:, 0:wb].astype(BF16)
    uf_ref[...] = z[:, wb:2 * wb].astype(BF16)
    up_ref[...] = z[:, 2 * wb:3 * wb]
    v_ref[...] = z[:, 3 * wb:4 * wb] * jax.nn.sigmoid(z[:, 4 * wb:5 * wb])


def _pre_call(x, pos, mod, g1, w5, *, mod_row, tn):
    b, s, d = x.shape
    wb = W_BRANCH
    has_pos = pos is not None
    tok = lambda bi, i: (bi, i, 0)
    const = lambda bi, i: (0, 0)
    in_specs = [pl.BlockSpec((None, tn, d), tok)]
    args = [x]
    if has_pos:
        in_specs.append(pl.BlockSpec((tn, d), lambda bi, i: (i, 0)))
        args.append(pos)
    in_specs += [
        pl.BlockSpec(mod.shape, const),
        pl.BlockSpec((1, d), const),
        pl.BlockSpec(w5.shape, const),
    ]
    args += [mod, g1, w5]
    out_spec = pl.BlockSpec((None, tn, wb), tok)
    return pl.pallas_call(
        functools.partial(_pre_kernel, has_pos=has_pos, mod_row=mod_row),
        grid=(b, s // tn),
        in_specs=in_specs,
        out_specs=[out_spec] * 4,
        out_shape=[
            jax.ShapeDtypeStruct((b, s, wb), BF16),
            jax.ShapeDtypeStruct((b, s, wb), BF16),
            jax.ShapeDtypeStruct((b, s, wb), F32),
            jax.ShapeDtypeStruct((b, s, wb), F32),
        ],
        compiler_params=pltpu.CompilerParams(
            dimension_semantics=("parallel", "parallel"), vmem_limit_bytes=VMEM_LIMIT),
        name="pre",
    )(*args)


def _s5_kernel(u_ref, e_ref, w2_ref, ar_ref, ai_ref, y_ref, sre, sim, *, n_ctx_chunks):
    groups, nc, _ = u_ref.shape
    half = LANES // 2

    for g in range(groups):
        e = _bdot(u_ref[g], e_ref[g])
        sre[pl.ds(g, nc, stride=groups), :] = e[:, 0:LANES]
        sim[pl.ds(g, nc, stride=groups), :] = e[:, LANES:2 * LANES]

    a_re = ar_ref[...]
    a_im = ai_ref[...]
    is_fwd = lax.broadcasted_iota(jnp.int32, (groups, LANES), 1) < half

    def step(i, carry):
        s_re, s_im = carry
        kf = i
        kr = jnp.where(i < n_ctx_chunks, n_ctx_chunks - 1 - i, nc - 1 + n_ctx_chunks - i)
        rf = pl.multiple_of(kf * groups, groups)
        rr = pl.multiple_of(kr * groups, groups)
        e_re = jnp.where(is_fwd, sre[pl.ds(rf, groups), :], sre[pl.ds(rr, groups), :])
        e_im = jnp.where(is_fwd, sim[pl.ds(rf, groups), :], sim[pl.ds(rr, groups), :])
        sre[pl.ds(rf, groups), 0:half] = s_re[:, 0:half]
        sim[pl.ds(rf, groups), 0:half] = s_im[:, 0:half]
        sre[pl.ds(rr, groups), half:LANES] = s_re[:, half:LANES]
        sim[pl.ds(rr, groups), half:LANES] = s_im[:, half:LANES]
        n_re = a_re * s_re - a_im * s_im + e_re
        n_im = a_re * s_im + a_im * s_re + e_im
        return n_re, n_im

    zero = jnp.zeros((groups, LANES), F32)
    lax.fori_loop(0, nc, step, (zero, zero))

    for g in range(groups):
        s_re = sre[pl.ds(g, nc, stride=groups), :].astype(BF16)
        s_im = sim[pl.ds(g, nc, stride=groups), :].astype(BF16)
        lhs = jnp.concatenate([u_ref[g], s_re, s_im], axis=1)
        y_ref[g] = _bdot(lhs, w2_ref[g]).astype(y_ref.dtype)


def _s5_call(u, e_mat, w2, a_re, a_im, *, n_ctx_chunks):
    b, groups, nc, lw = u.shape
    full3 = lambda bi: (0, 0, 0)
    return pl.pallas_call(
        functools.partial(_s5_kernel, n_ctx_chunks=n_ctx_chunks),
        grid=(b,),
        in_specs=[
            pl.BlockSpec((None, groups, nc, lw), lambda bi: (bi, 0, 0, 0)),
            pl.BlockSpec(e_mat.shape, full3),
            pl.BlockSpec(w2.shape, full3),
            pl.BlockSpec(a_re.shape, lambda bi: (0, 0)),
            pl.BlockSpec(a_im.shape, lambda bi: (0, 0)),
        ],
        out_specs=pl.BlockSpec((None, groups, nc, lw), lambda bi: (bi, 0, 0, 0)),
        out_shape=jax.ShapeDtypeStruct((b, groups, nc, lw), BF16),
        scratch_shapes=[pltpu.VMEM((nc * groups, LANES), F32), pltpu.VMEM((nc * groups, LANES), F32)],
        compiler_params=pltpu.CompilerParams(
            dimension_semantics=("parallel",), vmem_limit_bytes=VMEM_LIMIT),
        name="s5",
    )(u, e_mat, w2, a_re, a_im)


def _s5_prep(lam_re, lam_im, log_dt, b_re, b_im, c_re, c_im, d):
    t = S5_CHUNK
    g, p, cg = S5_GROUPS, S5_STATE, S5_GROUP
    hi = lax.Precision.HIGHEST
    lam_re = lam_re.astype(F32)
    lam_im = lam_im.astype(F32)
    dt = jnp.exp(log_dt.astype(F32))[..., None]
    mag = jnp.exp(lam_re * dt)
    ang = lam_im * dt
    a_re = mag * jnp.cos(ang)
    a_im = mag * jnp.sin(ang)
    den = lam_re * lam_re + lam_im * lam_im
    f_re = ((a_re - 1) * lam_re + a_im * lam_im) / den
    f_im = (a_im * lam_re - (a_re - 1) * lam_im) / den
    b_re = b_re.astype(F32)
    b_im = b_im.astype(F32)
    bb_re = f_re[..., None] * b_re - f_im[..., None] * b_im
    bb_im = f_re[..., None] * b_im + f_im[..., None] * b_re

    pows_re = [jnp.ones_like(a_re)]
    pows_im = [jnp.zeros_like(a_im)]
    for _ in range(t):
        pr, pi = pows_re[-1], pows_im[-1]
        pows_re.append(pr * a_re - pi * a_im)
        pows_im.append(pr * a_im + pi * a_re)
    pw_re = jnp.stack(pows_re, axis=2)
    pw_im = jnp.stack(pows_im, axis=2)

    c_re = c_re.astype(F32)
    c_im = c_im.astype(F32)
    ca_re = c_re[:, :, None] * pw_re[:, :, :, None, :] - c_im[:, :, None] * pw_im[:, :, :, None, :]
    ca_im = c_re[:, :, None] * pw_im[:, :, :, None, :] + c_im[:, :, None] * pw_re[:, :, :, None, :]
    kern = (jnp.einsum('kgtop,kgpc->kgtoc', ca_re[:, :, :t], bb_re, precision=hi)
            - jnp.einsum('kgtop,kgpc->kgtoc', ca_im[:, :, :t], bb_im, precision=hi))

    jj = np.arange(t)[:, None]
    ii = np.arange(t)[None, :]
    lag_f = np.clip(ii - jj, 0, t - 1)
    lag_r = np.clip(jj - ii, 0, t - 1)
    m_f = jnp.where((ii >= jj)[None, :, :, None, None], kern[0][:, lag_f], 0.0)
    m_r = jnp.where((jj >= ii)[None, :, :, None, None], kern[1][:, lag_r], 0.0)
    eye_t = np.eye(t, dtype=np.float32)[None, :, :, None, None]
    eye_c = jnp.eye(cg, dtype=F32)[None, None, None] * d.astype(F32).reshape(g, 1, 1, 1, cg)
    m = m_f + m_r + eye_t * eye_c
    m = m.transpose(0, 1, 4, 2, 3).reshape(g, t * cg, t * cg)

    def readout(k, powers):
        rr = ca_re[k][:, powers].transpose(0, 3, 1, 2).reshape(g, p, t * cg)
        ri = -ca_im[k][:, powers].transpose(0, 3, 1, 2).reshape(g, p, t * cg)
        return rr, ri

    rf_re, rf_im = readout(0, np.arange(1, t + 1))
    rr_re, rr_im = readout(1, t - np.arange(t))
    w2 = jnp.concatenate([m, rf_re, rr_re, rf_im, rr_im], axis=1)

    def endstate(k, powers):
        pr = pw_re[k][:, powers][..., None]
        pi = pw_im[k][:, powers][..., None]
        br = bb_re[k][:, None]
        bi = bb_im[k][:, None]
        er = (pr * br - pi * bi).transpose(0, 1, 3, 2).reshape(g, t * cg, p)
        ei = (pr * bi + pi * br).transpose(0, 1, 3, 2).reshape(g, t * cg, p)
        return er, ei

    ef_re, ef_im = endstate(0, t - 1 - np.arange(t))
    er_re, er_im = endstate(1, np.arange(t))
    e_mat = jnp.concatenate([ef_re, er_re, ef_im, er_im], axis=2)

    alpha_re = jnp.concatenate([pw_re[0][:, t], pw_re[1][:, t]], axis=-1)
    alpha_im = jnp.concatenate([pw_im[0][:, t], pw_im[1][:, t]], axis=-1)
    return e_mat.astype(BF16), w2.astype(BF16), alpha_re, alpha_im


def _to_chunk_layout(u):
    b, l, _ = u.shape
    t = S5_CHUNK
    u = u.reshape(b, l // t, t, S5_GROUPS, S5_GROUP).transpose(0, 3, 1, 2, 4)
    return u.reshape(b, S5_GROUPS, l // t, t * S5_GROUP)


def _from_chunk_layout(y):
    b, g, nc, _ = y.shape
    t = S5_CHUNK
    y = y.reshape(b, g, nc, t, S5_GROUP).transpose(0, 2, 3, 1, 4)
    return y.reshape(b, nc * t, g * S5_GROUP)


def _cmul_const(xr, xi, c, s):
    def close(a, b):
        return abs(a - b) < 1e-12
    if close(c, 1) and close(s, 0):
        return xr, xi
    if close(c, -1) and close(s, 0):
        return -xr, -xi
    if close(c, 0) and close(s, 1):
        return -xi, xr
    if close(c, 0) and close(s, -1):
        return xi, -xr
    return c * xr - s * xi, c * xi + s * xr


def _fft_list(xs):
    n = len(xs)
    if n == 1:
        return xs
    ev = _fft_list(xs[0::2])
    od = _fft_list(xs[1::2])
    out = [None] * n
    for k in range(n // 2):
        ang = -2.0 * math.pi * k / n
        tr, ti = _cmul_const(od[k][0], od[k][1], math.cos(ang), math.sin(ang))
        out[k] = (ev[k][0] + tr, ev[k][1] + ti)
        out[k + n // 2] = (ev[k][0] - tr, ev[k][1] - ti)
    return out


def _fnet_kernel(x_ref, f_ref, tc_ref, ts_ref, o_ref):
    wb = W_BRANCH
    x = x_ref[...]
    a_re = _bdot(f_ref[0].astype(BF16), x)
    a_im = _bdot(f_ref[1].astype(BF16), x)
    tc = tc_ref[...]
    ts = ts_ref[...]
    b_re = a_re * tc + a_im * ts
    b_im = a_im * tc - a_re * ts
    xs = [(b_re[:, n * wb:(n + 1) * wb], b_im[:, n * wb:(n + 1) * wb]) for n in range(FFT_RADIX)]
    zs = _fft_list(xs)
    for k2 in range(FFT_RADIX):
        o_ref[k2, :, 0:wb] = zs[k2][0].astype(o_ref.dtype)
        o_ref[k2, :, wb:2 * wb] = zs[k2][1].astype(o_ref.dtype)


@functools.lru_cache(maxsize=None)
def _fnet_consts(n1):
    n = n1 * FFT_RADIX
    k = np.arange(n1, dtype=np.int64)
    ang1 = 2.0 * np.pi * ((k[:, None] * k[None, :]) % n1) / n1
    f = np.stack([np.cos(ang1), -np.sin(ang1)]).astype(np.float32)
    n2 = np.arange(FFT_RADIX, dtype=np.int64)
    ang2 = 2.0 * np.pi * ((k[:, None] * n2[None, :]) % n) / n
    tc = np.repeat(np.cos(ang2), W_BRANCH, axis=1).astype(np.float32)
    ts = np.repeat(np.sin(ang2), W_BRANCH, axis=1).astype(np.float32)
    c = np.arange(FNET_GROUP, dtype=np.int64)
    angc = 2.0 * np.pi * ((c[:, None] * c[None, :]) % FNET_GROUP) / FNET_GROUP
    norm = 1.0 / math.sqrt(n * FNET_GROUP)
    ng = W_BRANCH // FNET_GROUP
    wcs = np.zeros((2 * W_BRANCH, W_BRANCH), np.float32)
    for gi in range(ng):
        sl = slice(gi * FNET_GROUP, (gi + 1) * FNET_GROUP)
        wcs[sl, sl] = np.cos(angc) * norm
        wcs[W_BRANCH + gi * FNET_GROUP:W_BRANCH + (gi + 1) * FNET_GROUP, sl] = np.sin(angc) * norm
    return f, tc, ts, wcs


def _fnet_call(uf):
    b, l, wb = uf.shape
    n1 = l // FFT_RADIX
    tk = min(n1, 256)
    f, tc, ts, _ = _fnet_consts(n1)
    x = uf.reshape(b, n1, FFT_RADIX * wb)
    out = pl.pallas_call(
        _fnet_kernel,
        grid=(b, n1 // tk),
        in_specs=[
            pl.BlockSpec((None, n1, FFT_RADIX * wb), lambda bi, i: (bi, 0, 0)),
            pl.BlockSpec((2, tk, n1), lambda bi, i: (0, i, 0)),
            pl.BlockSpec((tk, FFT_RADIX * wb), lambda bi, i: (i, 0)),
            pl.BlockSpec((tk, FFT_RADIX * wb), lambda bi, i: (i, 0)),
        ],
        out_specs=pl.BlockSpec((None, FFT_RADIX, tk, 2 * wb), lambda bi, i: (bi, 0, i, 0)),
        out_shape=jax.ShapeDtypeStruct((b, FFT_RADIX, n1, 2 * wb), BF16),
        compiler_params=pltpu.CompilerParams(
            dimension_semantics=("parallel", "parallel"), vmem_limit_bytes=VMEM_LIMIT),
        name="fnet",
    )(x, jnp.asarray(f), jnp.asarray(tc), jnp.asarray(ts))
    return out.reshape(b, l, 2 * wb)


def _merge_kernel(*refs, seq_len, tn, has_pos, final, mod_row):
    refs = list(refs)
    x_ref = refs.pop(0)
    pos_ref = refs.pop(0) if has_pos else None
    mod_ref, g1_ref, g2_ref = refs[0:3]
    refs = refs[3:]
    gf_ref = refs.pop(0) if final else None
    (ys_ref, z_ref, up_prev, up_cur, up_next, v_prev, v_cur, v_next,
     wglu_ref, wcs_ref, wfn_ref, wpool_ref, pscale_ref, cw_ref, cb_ref, lng_ref, lnb_ref, wcv_ref,
     wgate_ref, wbr_ref, wout_ref, w1_ref, w2_ref, o_ref, pbuf, vbuf) = refs

    d = D_MODEL
    i = pl.program_id(1)
    nt = pl.num_programs(1)
    x = x_ref[...]
    if has_pos:
        x = x + pos_ref[...]
    row = pl.program_id(0) if mod_row is None else mod_row
    sh1, sc1, ga1, sh2, sc2, ga2 = _mod_rows(mod_ref, row, N_MOD)
    h = (_rms(x, g1_ref[...]) * (1.0 + sc1) + sh1).astype(BF16)

    def gated_proj(k, branch):
        gate = jax.nn.sigmoid(_bdot(h, wgate_ref[:, k * d:(k + 1) * d]))
        return gate * _bdot(branch.astype(BF16), wbr_ref[k])

    y = jax.nn.gelu(ys_ref[...].astype(F32))
    merged = gated_proj(0, y * jax.nn.sigmoid(_bdot(y.astype(BF16), wglu_ref[...])))

    yf = _bdot(z_ref[...], wcs_ref[...].astype(BF16))
    merged += gated_proj(1, _bdot(yf.astype(BF16), wfn_ref[...]))

    first = i == 0
    last = i == nt - 1
    for buf, prev, cur, nxt in ((pbuf, up_prev, up_cur, up_next), (vbuf, v_prev, v_cur, v_next)):
        buf[0:HALO, :] = jnp.where(first, 0.0, prev[...])
        buf[HALO:HALO + tn, :] = cur[...]
        buf[HALO + tn:2 * HALO + tn, :] = jnp.where(last, 0.0, nxt[...])

    t_pos = i * tn + lax.broadcasted_iota(jnp.int32, (tn, LANES), 0)
    lane = lax.broadcasted_iota(jnp.int32, (tn, LANES), 1)
    low = lane < LANES // 2

    def count(half):
        return (jnp.minimum(t_pos + half, seq_len) - jnp.maximum(t_pos - half, 0)).astype(F32)

    def window(col, offsets):
        acc = None
        for off in offsets:
            part = pbuf[pl.ds(HALO + off, tn), col * LANES:(col + 1) * LANES]
            acc = part if acc is None else acc + part
        return acc

    pooled = []
    for col in range(2):
        h_small, h_big = POOL_HALF[2 * col], POOL_HALF[2 * col + 1]
        s_small = window(col, range(-h_small, h_small))
        s_big = s_small + window(col, list(range(-h_big, -h_small)) + list(range(h_small, h_big)))
        mean = jnp.where(low, s_small / count(h_small), s_big / count(h_big))
        pooled.append(mean - up_cur[:, col * LANES:(col + 1) * LANES])
    pooled = jnp.concatenate(pooled, axis=1).astype(BF16)
    merged += gated_proj(2, _bdot(pooled, wpool_ref[...]) * pscale_ref[...])

    acc = None
    for k in range(CONV_WIDTH):
        term = cw_ref[pl.ds(k, 1), :] * vbuf[pl.ds(HALO - CONV_WIDTH // 2 + k, tn), :]
        acc = term if acc is None else acc + term
    acc = acc + cb_ref[...]
    mu = jnp.mean(acc, axis=-1, keepdims=True)
    xc = acc - mu
    yn = xc * lax.rsqrt(jnp.mean(xc * xc, axis=-1, keepdims=True) + EPS) * lng_ref[...] + lnb_ref[...]
    yn = yn * jax.nn.sigmoid(yn)
    merged += gated_proj(3, _bdot(yn.astype(BF16), wcv_ref[...]))

    x1 = x + ga1 * _bdot(merged.astype(BF16), wout_ref[...])

    h2 = (_rms(x1, g2_ref[...]) * (1.0 + sc2) + sh2).astype(BF16)
    acc2 = None
    for cidx in range(D_FF // d):
        a = jnp.maximum(_bdot(h2, w1_ref[:, cidx * d:(cidx + 1) * d]), 0.0)
        part = _bdot((a * a).astype(BF16), w2_ref[cidx * d:(cidx + 1) * d, :])
        acc2 = part if acc2 is None else acc2 + part
    x2 = x1 + ga2 * acc2
    if final:
        x2 = _rms(x2, gf_ref[...])
    o_ref[...] = x2


def _merge_call(x, pos, mod, g1, g2, gf, ys, z, up, v, small, big, *, mod_row, tn):
    b, s, d = x.shape
    wb = W_BRANCH
    has_pos = pos is not None
    final = gf is not None
    nh = tn // HALO
    tok = lambda bi, i: (bi, i, 0)
    prev = lambda bi, i: (bi, jnp.maximum(i * nh - 1, 0), 0)
    nxt = lambda bi, i: (bi, jnp.minimum((i + 1) * nh, s // HALO - 1), 0)

    def const_spec(a):
        nd = a.ndim
        return pl.BlockSpec(a.shape, lambda bi, i: (0,) * nd, pipeline_mode=pl.Buffered(1))

    in_specs = [pl.BlockSpec((None, tn, d), tok)]
    args = [x]
    if has_pos:
        in_specs.append(pl.BlockSpec((tn, d), lambda bi, i: (i, 0)))
        args.append(pos)
    for a in (mod, g1, g2) + ((gf,) if final else ()):
        in_specs.append(const_spec(a))
        args.append(a)
    in_specs += [pl.BlockSpec((None, tn, wb), tok), pl.BlockSpec((None, tn, 2 * wb), tok)]
    args += [ys, z]
    for a in (up, v):
        in_specs += [pl.BlockSpec((None, HALO, wb), prev), pl.BlockSpec((None, tn, wb), tok),
                     pl.BlockSpec((None, HALO, wb), nxt)]
        args += [a, a, a]
    for a in tuple(small) + tuple(big):
        in_specs.append(const_spec(a))
        args.append(a)

    return pl.pallas_call(
        functools.partial(_merge_kernel, seq_len=s, tn=tn, has_pos=has_pos, final=final,
                          mod_row=mod_row),
        grid=(b, s // tn),
        in_specs=in_specs,
        out_specs=pl.BlockSpec((None, tn, d), tok),
        out_shape=jax.ShapeDtypeStruct((b, s, d), F32),
        scratch_shapes=[pltpu.VMEM((tn + 2 * HALO, wb), F32), pltpu.VMEM((tn + 2 * HALO, wb), F32)],
        compiler_params=pltpu.CompilerParams(
            dimension_semantics=("parallel", "parallel"), vmem_limit_bytes=VMEM_LIMIT),
        name="merge",
    )(*args)


def _pos_embed_2d(rows):
    row = jnp.repeat(jnp.arange(rows), GRID_W)
    col = jnp.tile(jnp.arange(GRID_W), rows)
    quarter = D_MODEL // 4
    freq = 1.0 / (POS_BASE ** (jnp.arange(quarter, dtype=F32) / quarter))

    def enc(p):
        ang = p.astype(F32)[:, None] * freq[None, :]
        return jnp.concatenate([jnp.sin(ang), jnp.cos(ang)], axis=-1)

    return jnp.concatenate([enc(row), enc(col)], axis=-1)


def _block_diag(w):
    g, a, b = w.shape
    out = jnp.zeros((g * a, g * b), w.dtype)
    for k in range(g):
        out = out.at[k * a:(k + 1) * a, k * b:(k + 1) * b].set(w[k])
    return out


def kernel(x, c, ctx, c_ctx, w_mod, b_mod, g_norm1, w_in, s5_lam_re, s5_lam_im, s5_log_dt, s5_b_re, s5_b_im, s5_c_re, s5_c_im, s5_d, s5_w_glu, fnet_w, pool_w, pool_scale, conv_w, conv_b, conv_ln_g, conv_ln_b, conv_w_out, w_branch, w_out, g_norm2, mlp_w1, mlp_w2, g_final):
    bsz, seq, d = x.shape
    ctx_len = ctx.shape[1]
    depth = w_mod.shape[0]
    wb = W_BRANCH
    assert bsz < 8 and seq % 512 == 0 and ctx_len % (S5_CHUNK * 8) == 0

    pos = _pos_embed_2d(seq // GRID_W)
    cpad = jnp.zeros((8, d), F32).at[:bsz].set(c).at[bsz].set(c_ctx)
    mod_all = _mod_call(cpad, w_mod, b_mod)
    ctx_row = bsz
    row2 = lambda a: a.reshape(1, -1).astype(F32)

    xc = ctx
    for l in range(depth):
        last = l == depth - 1
        mod = mod_all[l]
        g1 = row2(g_norm1[l])
        g2 = row2(g_norm2[l])
        w5 = w_in[l][:, :5 * wb].astype(BF16)
        e_mat, w2s, a_re, a_im = _s5_prep(s5_lam_re[l], s5_lam_im[l], s5_log_dt[l], s5_b_re[l],
                                          s5_b_im[l], s5_c_re[l], s5_c_im[l], s5_d[l])
        cw = jnp.zeros((32, wb), F32).at[:CONV_WIDTH].set(conv_w[l])
        small = (s5_w_glu[l].astype(BF16), None, fnet_w[l].astype(BF16),
                 _block_diag(pool_w[l]).astype(BF16), row2(pool_scale[l]), cw, row2(conv_b[l]),
                 row2(conv_ln_g[l]), row2(conv_ln_b[l]), conv_w_out[l].astype(BF16))
        big = (w_in[l][:, 5 * wb:].astype(BF16), w_branch[l].astype(BF16), w_out[l].astype(BF16),
               mlp_w1[l].astype(BF16), mlp_w2[l].astype(BF16))

        def with_wcs(n):
            wcs = jnp.asarray(_fnet_consts(n // FFT_RADIX)[3])
            return small[:1] + (wcs,) + small[2:]

        us_c, uf_c, up_c, v_c = _pre_call(xc, None, mod, g1, w5, mod_row=ctx_row, tn=ctx_len)
        us, uf, up, v = _pre_call(x, pos if l == 0 else None, mod, g1, w5, mod_row=None, tn=512)

        u_all = _to_chunk_layout(jnp.concatenate([us_c, us], axis=1))
        y_all = _from_chunk_layout(
            _s5_call(u_all, e_mat, w2s, a_re, a_im, n_ctx_chunks=ctx_len // S5_CHUNK))
        ys_c, ys = y_all[:, :ctx_len], y_all[:, ctx_len:]

        z = _fnet_call(uf)
        x = _merge_call(x, pos if l == 0 else None, mod, g1, g2, row2(g_final) if last else None,
                        ys, z, up, v, with_wcs(seq), big, mod_row=None, tn=512)
        if not last:
            z_c = _fnet_call(uf_c)
            xc = _merge_call(xc, None, mod, g1, g2, None, ys_c, z_c, up_c, v_c, with_wcs(ctx_len),
                             big, mod_row=ctx_row, tn=ctx_len)
    return x
```

```python
import functools
import math

import numpy as np
import jax
import jax.numpy as jnp
from jax import lax
from jax.experimental import pallas as pl
from jax.experimental.pallas import tpu as pltpu

F32 = jnp.float32
BF16 = jnp.bfloat16

D_MODEL = 1024
W_BRANCH = 256
N_BRANCH = 4
N_MOD = 6
D_FF = 4 * D_MODEL
S5_GROUPS = 16
S5_GROUP = 16
S5_STATE = 64
FNET_GROUP = 64
GRID_W = 64
CONV_WIDTH = 31
POOL_HALF = (1, 2, 4, 8)
EPS = 1e-6
POS_BASE = 10000.0

S5_CHUNK = 16
FFT_RADIX = 8
HALO = 16
LANES = 128
VMEM_LIMIT = 56 * 1024 * 1024


def _bdot(a, b):
    return jnp.dot(a, b, preferred_element_type=F32)


def _rms(x, g):
    return x * lax.rsqrt(jnp.mean(x * x, axis=-1, keepdims=True) + EPS) * g


def _mod_rows(mod_ref, row, n):
    return [mod_ref[pl.ds(row, 1), k * D_MODEL:(k + 1) * D_MODEL] for k in range(n)]


def _mod_kernel(c_ref, w_ref, b_ref, o_ref):
    c = c_ref[...]
    act = c * jax.nn.sigmoid(c)
    o_ref[...] = jnp.dot(act, w_ref[...], preferred_element_type=F32,
                         precision=lax.Precision.HIGHEST) + b_ref[...]


def _mod_call(cpad, w_mod, b_mod):
    depth, d, n = w_mod.shape
    tc = 1536
    return pl.pallas_call(
        _mod_kernel,
        grid=(depth, n // tc),
        in_specs=[
            pl.BlockSpec((8, d), lambda l, j: (0, 0)),
            pl.BlockSpec((None, d, tc), lambda l, j: (l, 0, j)),
            pl.BlockSpec((None, 1, tc), lambda l, j: (l, 0, j)),
        ],
        out_specs=pl.BlockSpec((None, 8, tc), lambda l, j: (l, 0, j)),
        out_shape=jax.ShapeDtypeStruct((depth, 8, n), F32),
        compiler_params=pltpu.CompilerParams(
            dimension_semantics=("parallel", "parallel"), vmem_limit_bytes=VMEM_LIMIT),
        name="mod",
    )(cpad, w_mod, b_mod.reshape(depth, 1, n))


def _pos_tile(rowtab_ref, coltile_ref, tile_idx, tn):
    rows_per_tile = tn // GRID_W
    half = D_MODEL // 2
    parts = [jnp.broadcast_to(rowtab_ref[pl.ds(tile_idx * rows_per_tile + q, 1), :], (GRID_W, half))
             for q in range(rows_per_tile)]
    return jnp.concatenate([jnp.concatenate(parts, axis=0), coltile_ref[...]], axis=1)


def _pre_kernel(*refs, has_pos, mod_row):
    if has_pos:
        (x_ref, rowtab_ref, coltile_ref, mod_ref, g1_ref, w5_ref,
         us_ref, uf_ref, up_ref, v_ref, fbuf) = refs
        x = x_ref[...] + _pos_tile(rowtab_ref, coltile_ref, pl.program_id(1), x_ref.shape[0])
    else:
        x_ref, mod_ref, g1_ref, w5_ref, us_ref, uf_ref, up_ref, v_ref, fbuf = refs
        x = x_ref[...]
    row = pl.program_id(0) if mod_row is None else mod_row
    sh, sc = _mod_rows(mod_ref, row, 2)
    h = (_rms(x, g1_ref[...]) * (1.0 + sc) + sh).astype(BF16)
    z = _bdot(h, w5_ref[...])
    wb = W_BRANCH
    tn = z.shape[0]
    us_ref[0] = z[:, 0:LANES]
    us_ref[1] = z[:, LANES:wb]
    fbuf[0] = z[:, wb:wb + LANES]
    fbuf[1] = z[:, wb + LANES:2 * wb]
    for n2 in range(FFT_RADIX):
        for hf in range(2):
            lo = n2 * wb + hf * LANES
            uf_ref[:, lo:lo + LANES] = fbuf[hf, pl.ds(n2, tn // FFT_RADIX, stride=FFT_RADIX), :].astype(BF16)
    up_ref[...] = z[:, 2 * wb:3 * wb]
    v_ref[...] = z[:, 3 * wb:4 * wb] * jax.nn.sigmoid(z[:, 4 * wb:5 * wb])


def _pre_call(x, pos_tabs, mod, g1, w5, *, mod_row, tn):
    b, s, d = x.shape
    wb = W_BRANCH
    has_pos = pos_tabs is not None
    tok = lambda bi, i: (bi, i, 0)
    const = lambda bi, i: (0, 0)
    in_specs = [pl.BlockSpec((None, tn, d), tok)]
    args = [x]
    if has_pos:
        in_specs += [pl.BlockSpec(a.shape, const) for a in pos_tabs]
        args += list(pos_tabs)
    in_specs += [
        pl.BlockSpec(mod.shape, const),
        pl.BlockSpec((1, d), const),
        pl.BlockSpec(w5.shape, const),
    ]
    args += [mod, g1, w5]
    out_spec = pl.BlockSpec((None, tn, wb), tok)
    return pl.pallas_call(
        functools.partial(_pre_kernel, has_pos=has_pos, mod_row=mod_row),
        grid=(b, s // tn),
        in_specs=in_specs,
        out_specs=[
            pl.BlockSpec((None, 2, tn, LANES), lambda bi, i: (bi, 0, i, 0)),
            pl.BlockSpec((None, tn // FFT_RADIX, FFT_RADIX * wb), tok),
            out_spec, out_spec,
        ],
        out_shape=[
            jax.ShapeDtypeStruct((b, 2, s, LANES), F32),
            jax.ShapeDtypeStruct((b, s // FFT_RADIX, FFT_RADIX * wb), BF16),
            jax.ShapeDtypeStruct((b, s, wb), F32),
            jax.ShapeDtypeStruct((b, s, wb), F32),
        ],
        scratch_shapes=[pltpu.VMEM((2, tn, LANES), F32)],
        compiler_params=pltpu.CompilerParams(
            dimension_semantics=("parallel", "parallel"), vmem_limit_bytes=VMEM_LIMIT),
        name="pre",
    )(*args)


def _gather_lane_block(pieces, sel):
    blk = lax.broadcasted_iota(jnp.int32, pieces[0].shape, 1) // S5_GROUP
    out = None
    for q, p in enumerate(pieces):
        shift = ((q - sel) % 8) * S5_GROUP
        r = pltpu.roll(p, shift, axis=1) if shift else p
        out = r if out is None else jnp.where(blk == q, r, out)
    return out


def _s5_kernel(usc_ref, us_ref, e_ref, w2_ref, ar_ref, ai_ref, yc_ref, y_ref, ubuf, sre, sim, ybuf,
               *, n_ctx_chunks):
    groups, nc, _ = ubuf.shape
    t = S5_CHUNK
    half = LANES // 2
    n_lat_chunks = nc - n_ctx_chunks
    rb = 64

    def relayout_in(src_ref, src_chunk0, nrows, dst_row0):
        for hf in range(2):
            for jj in range(2):
                pieces = [src_ref[hf, pl.ds(src_chunk0 * t + jj * 8 + m, nrows, stride=t), :]
                          for m in range(8)]
                for g8 in range(8):
                    col = _gather_lane_block(pieces, g8).astype(BF16)
                    ubuf[hf * 8 + g8, pl.ds(dst_row0, nrows), jj * LANES:(jj + 1) * LANES] = col

    relayout_in(usc_ref, 0, n_ctx_chunks, 0)

    def in_body(i, carry):
        r0 = pl.multiple_of(i * rb, rb)
        relayout_in(us_ref, r0, rb, pl.multiple_of(n_ctx_chunks + r0, 16))
        return carry

    lax.fori_loop(0, n_lat_chunks // rb, in_body, 0)

    for g in range(groups):
        e = _bdot(ubuf[g], e_ref[g])
        sre[pl.ds(g, nc, stride=groups), :] = e[:, 0:LANES]
        sim[pl.ds(g, nc, stride=groups), :] = e[:, LANES:2 * LANES]

    a_re = ar_ref[...]
    a_im = ai_ref[...]
    is_fwd = lax.broadcasted_iota(jnp.int32, (groups, LANES), 1) < half

    def step(i, carry):
        s_re, s_im = carry
        kf = i
        kr = jnp.where(i < n_ctx_chunks, n_ctx_chunks - 1 - i, nc - 1 + n_ctx_chunks - i)
        rf = pl.multiple_of(kf * groups, groups)
        rr = pl.multiple_of(kr * groups, groups)
        e_re = jnp.where(is_fwd, sre[pl.ds(rf, groups), :], sre[pl.ds(rr, groups), :])
        e_im = jnp.where(is_fwd, sim[pl.ds(rf, groups), :], sim[pl.ds(rr, groups), :])
        sre[pl.ds(rf, groups), 0:half] = s_re[:, 0:half]
        sim[pl.ds(rf, groups), 0:half] = s_im[:, 0:half]
        sre[pl.ds(rr, groups), half:LANES] = s_re[:, half:LANES]
        sim[pl.ds(rr, groups), half:LANES] = s_im[:, half:LANES]
        n_re = a_re * s_re - a_im * s_im + e_re
        n_im = a_re * s_im + a_im * s_re + e_im
        return n_re, n_im

    zero = jnp.zeros((groups, LANES), F32)
    lax.fori_loop(0, nc, step, (zero, zero))

    for hf in range(2):
        for g8 in range(8):
            g = hf * 8 + g8
            s_re = sre[pl.ds(g, nc, stride=groups), :].astype(BF16)
            s_im = sim[pl.ds(g, nc, stride=groups), :].astype(BF16)
            lhs = jnp.concatenate([ubuf[g], s_re, s_im], axis=1)
            ybuf[g8] = _bdot(lhs, w2_ref[g])

        def relayout_out(dst_ref, src_row0, nrows, dst_chunk0):
            for ii in range(2):
                pieces = [ybuf[q, pl.ds(src_row0, nrows), ii * LANES:(ii + 1) * LANES] for q in range(8)]
                for m in range(8):
                    tok = _gather_lane_block(pieces, m)
                    dst_ref[hf, pl.ds(dst_chunk0 * t + ii * 8 + m, nrows, stride=t), :] = tok

        relayout_out(yc_ref, 0, n_ctx_chunks, 0)

        def out_body(i, carry):
            r0 = pl.multiple_of(i * rb, rb)
            relayout_out(y_ref, pl.multiple_of(n_ctx_chunks + r0, 8), rb, r0)
            return carry

        lax.fori_loop(0, n_lat_chunks // rb, out_body, 0)


def _s5_call(us_c, us, e_mat, w2, a_re, a_im):
    b, _, lc, _ = us_c.shape
    s = us.shape[2]
    t = S5_CHUNK
    n_ctx_chunks = lc // t
    nc = n_ctx_chunks + s // t
    groups = S5_GROUPS
    lw = t * S5_GROUP
    single = pl.Buffered(1)
    slab = lambda n: pl.BlockSpec((None, 2, n, LANES), lambda bi: (bi, 0, 0, 0))
    slab1 = lambda n: pl.BlockSpec((None, 2, n, LANES), lambda bi: (bi, 0, 0, 0), pipeline_mode=single)
    full3 = lambda a: pl.BlockSpec(a.shape, lambda bi: (0, 0, 0), pipeline_mode=single)
    full2 = lambda a: pl.BlockSpec(a.shape, lambda bi: (0, 0))
    return pl.pallas_call(
        functools.partial(_s5_kernel, n_ctx_chunks=n_ctx_chunks),
        grid=(b,),
        in_specs=[slab1(lc), slab1(s), full3(e_mat), full3(w2), full2(a_re), full2(a_im)],
        out_specs=[slab(lc), slab(s)],
        out_shape=[jax.ShapeDtypeStruct((b, 2, lc, LANES), F32),
                   jax.ShapeDtypeStruct((b, 2, s, LANES), F32)],
        scratch_shapes=[pltpu.VMEM((groups, nc, lw), BF16),
                        pltpu.VMEM((nc * groups, LANES), F32),
                        pltpu.VMEM((nc * groups, LANES), F32),
                        pltpu.VMEM((8, nc, lw), F32)],
        compiler_params=pltpu.CompilerParams(
            dimension_semantics=("parallel",), vmem_limit_bytes=VMEM_LIMIT),
        name="s5",
    )(us_c, us, e_mat, w2, a_re, a_im)


def _s5_prep(lam_re, lam_im, log_dt, b_re, b_im, c_re, c_im, d):
    t = S5_CHUNK
    g, p, cg = S5_GROUPS, S5_STATE, S5_GROUP
    hi = lax.Precision.HIGHEST
    lam_re = lam_re.astype(F32)
    lam_im = lam_im.astype(F32)
    dt = jnp.exp(log_dt.astype(F32))[..., None]
    mag = jnp.exp(lam_re * dt)
    ang = lam_im * dt
    a_re = mag * jnp.cos(ang)
    a_im = mag * jnp.sin(ang)
    den = lam_re * lam_re + lam_im * lam_im
    f_re = ((a_re - 1) * lam_re + a_im * lam_im) / den
    f_im = (a_im * lam_re - (a_re - 1) * lam_im) / den
    b_re = b_re.astype(F32)
    b_im = b_im.astype(F32)
    bb_re = f_re[..., None] * b_re - f_im[..., None] * b_im
    bb_im = f_re[..., None] * b_im + f_im[..., None] * b_re

    pows_re = [jnp.ones_like(a_re)]
    pows_im = [jnp.zeros_like(a_im)]
    for _ in range(t):
        pr, pi = pows_re[-1], pows_im[-1]
        pows_re.append(pr * a_re - pi * a_im)
        pows_im.append(pr * a_im + pi * a_re)
    pw_re = jnp.stack(pows_re, axis=2)
    pw_im = jnp.stack(pows_im, axis=2)

    c_re = c_re.astype(F32)
    c_im = c_im.astype(F32)
    ca_re = c_re[:, :, None] * pw_re[:, :, :, None, :] - c_im[:, :, None] * pw_im[:, :, :, None, :]
    ca_im = c_re[:, :, None] * pw_im[:, :, :, None, :] + c_im[:, :, None] * pw_re[:, :, :, None, :]
    kern = (jnp.einsum('kgtop,kgpc->kgtoc', ca_re[:, :, :t], bb_re, precision=hi)
            - jnp.einsum('kgtop,kgpc->kgtoc', ca_im[:, :, :t], bb_im, precision=hi))

    jj = np.arange(t)[:, None]
    ii = np.arange(t)[None, :]
    lag_f = np.clip(ii - jj, 0, t - 1)
    lag_r = np.clip(jj - ii, 0, t - 1)
    m_f = jnp.where((ii >= jj)[None, :, :, None, None], kern[0][:, lag_f], 0.0)
    m_r = jnp.where((jj >= ii)[None, :, :, None, None], kern[1][:, lag_r], 0.0)
    eye_t = np.eye(t, dtype=np.float32)[None, :, :, None, None]
    eye_c = jnp.eye(cg, dtype=F32)[None, None, None] * d.astype(F32).reshape(g, 1, 1, 1, cg)
    m = m_f + m_r + eye_t * eye_c
    m = m.transpose(0, 1, 4, 2, 3).reshape(g, t * cg, t * cg)

    def readout(k, powers):
        rr = ca_re[k][:, powers].transpose(0, 3, 1, 2).reshape(g, p, t * cg)
        ri = -ca_im[k][:, powers].transpose(0, 3, 1, 2).reshape(g, p, t * cg)
        return rr, ri

    rf_re, rf_im = readout(0, np.arange(1, t + 1))
    rr_re, rr_im = readout(1, t - np.arange(t))
    w2 = jnp.concatenate([m, rf_re, rr_re, rf_im, rr_im], axis=1)

    def endstate(k, powers):
        pr = pw_re[k][:, powers][..., None]
        pi = pw_im[k][:, powers][..., None]
        br = bb_re[k][:, None]
        bi = bb_im[k][:, None]
        er = (pr * br - pi * bi).transpose(0, 1, 3, 2).reshape(g, t * cg, p)
        ei = (pr * bi + pi * br).transpose(0, 1, 3, 2).reshape(g, t * cg, p)
        return er, ei

    ef_re, ef_im = endstate(0, t - 1 - np.arange(t))
    er_re, er_im = endstate(1, np.arange(t))
    e_mat = jnp.concatenate([ef_re, er_re, ef_im, er_im], axis=2)

    alpha_re = jnp.concatenate([pw_re[0][:, t], pw_re[1][:, t]], axis=-1)
    alpha_im = jnp.concatenate([pw_im[0][:, t], pw_im[1][:, t]], axis=-1)
    return e_mat.astype(BF16), w2.astype(BF16), alpha_re, alpha_im


def _cmul_const(xr, xi, c, s):
    def close(a, b):
        return abs(a - b) < 1e-12
    if close(c, 1) and close(s, 0):
        return xr, xi
    if close(c, -1) and close(s, 0):
        return -xr, -xi
    if close(c, 0) and close(s, 1):
        return -xi, xr
    if close(c, 0) and close(s, -1):
        return xi, -xr
    return c * xr - s * xi, c * xi + s * xr


def _fft_list(xs):
    n = len(xs)
    if n == 1:
        return xs
    ev = _fft_list(xs[0::2])
    od = _fft_list(xs[1::2])
    out = [None] * n
    for k in range(n // 2):
        ang = -2.0 * math.pi * k / n
        tr, ti = _cmul_const(od[k][0], od[k][1], math.cos(ang), math.sin(ang))
        out[k] = (ev[k][0] + tr, ev[k][1] + ti)
        out[k + n // 2] = (ev[k][0] - tr, ev[k][1] - ti)
    return out


def _fnet_kernel(x_ref, f_ref, tc_ref, ts_ref, o_ref):
    wb = W_BRANCH
    x = x_ref[...]
    a_re = _bdot(f_ref[0].astype(BF16), x)
    a_im = _bdot(f_ref[1].astype(BF16), x)
    tc = tc_ref[...]
    ts = ts_ref[...]
    b_re = a_re * tc + a_im * ts
    b_im = a_im * tc - a_re * ts
    xs = [(b_re[:, n * wb:(n + 1) * wb], b_im[:, n * wb:(n + 1) * wb]) for n in range(FFT_RADIX)]
    zs = _fft_list(xs)
    for k2 in range(FFT_RADIX):
        o_ref[k2, :, 0:wb] = zs[k2][0].astype(o_ref.dtype)
        o_ref[k2, :, wb:2 * wb] = zs[k2][1].astype(o_ref.dtype)


@functools.lru_cache(maxsize=None)
def _fnet_consts(n1):
    n = n1 * FFT_RADIX
    k = np.arange(n1, dtype=np.int64)
    ang1 = 2.0 * np.pi * ((k[:, None] * k[None, :]) % n1) / n1
    f = np.stack([np.cos(ang1), -np.sin(ang1)]).astype(np.float32)
    n2 = np.arange(FFT_RADIX, dtype=np.int64)
    ang2 = 2.0 * np.pi * ((k[:, None] * n2[None, :]) % n) / n
    tc = np.repeat(np.cos(ang2), W_BRANCH, axis=1).astype(np.float32)
    ts = np.repeat(np.sin(ang2), W_BRANCH, axis=1).astype(np.float32)
    c = np.arange(FNET_GROUP, dtype=np.int64)
    angc = 2.0 * np.pi * ((c[:, None] * c[None, :]) % FNET_GROUP) / FNET_GROUP
    norm = 1.0 / math.sqrt(n * FNET_GROUP)
    ng = W_BRANCH // FNET_GROUP
    wcs = np.zeros((2 * W_BRANCH, W_BRANCH), np.float32)
    for gi in range(ng):
        sl = slice(gi * FNET_GROUP, (gi + 1) * FNET_GROUP)
        wcs[sl, sl] = np.cos(angc) * norm
        wcs[W_BRANCH + gi * FNET_GROUP:W_BRANCH + (gi + 1) * FNET_GROUP, sl] = np.sin(angc) * norm
    return f, tc, ts, wcs


def _fnet_call(x):
    b, n1, _ = x.shape
    wb = W_BRANCH
    l = n1 * FFT_RADIX
    tk = min(n1, 256)
    f, tc, ts, _ = _fnet_consts(n1)
    out = pl.pallas_call(
        _fnet_kernel,
        grid=(b, n1 // tk),
        in_specs=[
            pl.BlockSpec((None, n1, FFT_RADIX * wb), lambda bi, i: (bi, 0, 0)),
            pl.BlockSpec((2, tk, n1), lambda bi, i: (0, i, 0)),
            pl.BlockSpec((tk, FFT_RADIX * wb), lambda bi, i: (i, 0)),
            pl.BlockSpec((tk, FFT_RADIX * wb), lambda bi, i: (i, 0)),
        ],
        out_specs=pl.BlockSpec((None, FFT_RADIX, tk, 2 * wb), lambda bi, i: (bi, 0, i, 0)),
        out_shape=jax.ShapeDtypeStruct((b, FFT_RADIX, n1, 2 * wb), BF16),
        compiler_params=pltpu.CompilerParams(
            dimension_semantics=("parallel", "parallel"), vmem_limit_bytes=VMEM_LIMIT),
        name="fnet",
    )(x, jnp.asarray(f), jnp.asarray(tc), jnp.asarray(ts))
    return out.reshape(b, l, 2 * wb)


def _merge_kernel(*refs, seq_len, tn, has_pos, final, mod_row):
    refs = list(refs)
    x_ref = refs.pop(0)
    rowtab_ref, coltile_ref = (refs.pop(0), refs.pop(0)) if has_pos else (None, None)
    mod_ref, g1_ref, g2_ref = refs[0:3]
    refs = refs[3:]
    gf_ref = refs.pop(0) if final else None
    (ys_ref, z_ref, up_prev, up_cur, up_next, v_prev, v_cur, v_next,
     wglu_ref, wcs_ref, wfn_ref, wpool_ref, pscale_ref, cw_ref, cb_ref, lng_ref, lnb_ref, wcv_ref,
     wgate_ref, wbr_ref, wout_ref, w1_ref, w2_ref, o_ref, pbuf, vbuf) = refs

    d = D_MODEL
    i = pl.program_id(1)
    nt = pl.num_programs(1)
    x = x_ref[...]
    if has_pos:
        x = x + _pos_tile(rowtab_ref, coltile_ref, i, tn)
    row = pl.program_id(0) if mod_row is None else mod_row
    sh1, sc1, ga1, sh2, sc2, ga2 = _mod_rows(mod_ref, row, N_MOD)
    h = (_rms(x, g1_ref[...]) * (1.0 + sc1) + sh1).astype(BF16)

    def gated_proj(k, branch):
        gate = jax.nn.sigmoid(_bdot(h, wgate_ref[:, k * d:(k + 1) * d]))
        return gate * _bdot(branch.astype(BF16), wbr_ref[k])

    y = jax.nn.gelu(jnp.concatenate([ys_ref[0], ys_ref[1]], axis=1))
    merged = gated_proj(0, y * jax.nn.sigmoid(_bdot(y.astype(BF16), wglu_ref[...])))

    yf = _bdot(z_ref[...], wcs_ref[...].astype(BF16))
    merged += gated_proj(1, _bdot(yf.astype(BF16), wfn_ref[...]))

    first = i == 0
    last = i == nt - 1
    for buf, prev, cur, nxt in ((pbuf, up_prev, up_cur, up_next), (vbuf, v_prev, v_cur, v_next)):
        buf[0:HALO, :] = jnp.where(first, 0.0, prev[...])
        buf[HALO:HALO + tn, :] = cur[...]
        buf[HALO + tn:2 * HALO + tn, :] = jnp.where(last, 0.0, nxt[...])

    t_pos = i * tn + lax.broadcasted_iota(jnp.int32, (tn, LANES), 0)
    lane = lax.broadcasted_iota(jnp.int32, (tn, LANES), 1)
    low = lane < LANES // 2

    def count(half):
        return (jnp.minimum(t_pos + half, seq_len) - jnp.maximum(t_pos - half, 0)).astype(F32)

    def window(col, offsets):
        acc = None
        for off in offsets:
            part = pbuf[pl.ds(HALO + off, tn), col * LANES:(col + 1) * LANES]
            acc = part if acc is None else acc + part
        return acc

    pooled = []
    for col in range(2):
        h_small, h_big = POOL_HALF[2 * col], POOL_HALF[2 * col + 1]
        s_small = window(col, range(-h_small, h_small))
        s_big = s_small + window(col, list(range(-h_big, -h_small)) + list(range(h_small, h_big)))
        mean = jnp.where(low, s_small / count(h_small), s_big / count(h_big))
        pooled.append(mean - up_cur[:, col * LANES:(col + 1) * LANES])
    pooled = jnp.concatenate(pooled, axis=1).astype(BF16)
    merged += gated_proj(2, _bdot(pooled, wpool_ref[...]) * pscale_ref[...])

    acc = None
    for k in range(CONV_WIDTH):
        term = cw_ref[pl.ds(k, 1), :] * vbuf[pl.ds(HALO - CONV_WIDTH // 2 + k, tn), :]
        acc = term if acc is None else acc + term
    acc = acc + cb_ref[...]
    mu = jnp.mean(acc, axis=-1, keepdims=True)
    xc = acc - mu
    yn = xc * lax.rsqrt(jnp.mean(xc * xc, axis=-1, keepdims=True) + EPS) * lng_ref[...] + lnb_ref[...]
    yn = yn * jax.nn.sigmoid(yn)
    merged += gated_proj(3, _bdot(yn.astype(BF16), wcv_ref[...]))

    x1 = x + ga1 * _bdot(merged.astype(BF16), wout_ref[...])

    h2 = (_rms(x1, g2_ref[...]) * (1.0 + sc2) + sh2).astype(BF16)
    acc2 = None
    for cidx in range(D_FF // d):
        a = jnp.maximum(_bdot(h2, w1_ref[:, cidx * d:(cidx + 1) * d]), 0.0)
        part = _bdot((a * a).astype(BF16), w2_ref[cidx * d:(cidx + 1) * d, :])
        acc2 = part if acc2 is None else acc2 + part
    x2 = x1 + ga2 * acc2
    if final:
        x2 = _rms(x2, gf_ref[...])
    o_ref[...] = x2


def _merge_call(x, pos_tabs, mod, g1, g2, gf, ys, z, up, v, small, big, *, mod_row, tn):
    b, s, d = x.shape
    wb = W_BRANCH
    has_pos = pos_tabs is not None
    final = gf is not None
    nh = tn // HALO
    tok = lambda bi, i: (bi, i, 0)
    prev = lambda bi, i: (bi, jnp.maximum(i * nh - 1, 0), 0)
    nxt = lambda bi, i: (bi, jnp.minimum((i + 1) * nh, s // HALO - 1), 0)

    def const_spec(a):
        nd = a.ndim
        return pl.BlockSpec(a.shape, lambda bi, i: (0,) * nd, pipeline_mode=pl.Buffered(1))

    in_specs = [pl.BlockSpec((None, tn, d), tok)]
    args = [x]
    for a in (tuple(pos_tabs) if has_pos else ()) + (mod, g1, g2) + ((gf,) if final else ()):
        in_specs.append(const_spec(a))
        args.append(a)
    in_specs += [pl.BlockSpec((None, 2, tn, LANES), lambda bi, i: (bi, 0, i, 0)),
                 pl.BlockSpec((None, tn, 2 * wb), tok)]
    args += [ys, z]
    for a in (up, v):
        in_specs += [pl.BlockSpec((None, HALO, wb), prev), pl.BlockSpec((None, tn, wb), tok),
                     pl.BlockSpec((None, HALO, wb), nxt)]
        args += [a, a, a]
    for a in tuple(small) + tuple(big):
        in_specs.append(const_spec(a))
        args.append(a)

    return pl.pallas_call(
        functools.partial(_merge_kernel, seq_len=s, tn=tn, has_pos=has_pos, final=final,
                          mod_row=mod_row),
        grid=(b, s // tn),
        in_specs=in_specs,
        out_specs=pl.BlockSpec((None, tn, d), tok),
        out_shape=jax.ShapeDtypeStruct((b, s, d), F32),
        scratch_shapes=[pltpu.VMEM((tn + 2 * HALO, wb), F32), pltpu.VMEM((tn + 2 * HALO, wb), F32)],
        compiler_params=pltpu.CompilerParams(
            dimension_semantics=("parallel", "parallel"), vmem_limit_bytes=VMEM_LIMIT),
        name="merge",
    )(*args)


def _pos_tables(rows, tn):
    quarter = D_MODEL // 4
    freq = 1.0 / (POS_BASE ** (jnp.arange(quarter, dtype=F32) / quarter))

    def enc(p):
        ang = p.astype(F32)[:, None] * freq[None, :]
        return jnp.concatenate([jnp.sin(ang), jnp.cos(ang)], axis=-1)

    return enc(jnp.arange(rows)), enc(jnp.tile(jnp.arange(GRID_W), tn // GRID_W))


def _block_diag(w):
    g, a, b = w.shape
    out = jnp.zeros((g * a, g * b), w.dtype)
    for k in range(g):
        out = out.at[k * a:(k + 1) * a, k * b:(k + 1) * b].set(w[k])
    return out


def kernel(x, c, ctx, c_ctx, w_mod, b_mod, g_norm1, w_in, s5_lam_re, s5_lam_im, s5_log_dt, s5_b_re, s5_b_im, s5_c_re, s5_c_im, s5_d, s5_w_glu, fnet_w, pool_w, pool_scale, conv_w, conv_b, conv_ln_g, conv_ln_b, conv_w_out, w_branch, w_out, g_norm2, mlp_w1, mlp_w2, g_final):
    bsz, seq, d = x.shape
    ctx_len = ctx.shape[1]
    depth = w_mod.shape[0]
    wb = W_BRANCH
    assert bsz < 8 and seq % 512 == 0 and ctx_len % (S5_CHUNK * 8) == 0

    tn = 512
    pos_tabs = _pos_tables(seq // GRID_W, tn)
    cpad = jnp.zeros((8, d), F32).at[:bsz].set(c).at[bsz].set(c_ctx)
    mod_all = _mod_call(cpad, w_mod, b_mod)
    ctx_row = bsz
    row2 = lambda a: a.reshape(1, -1).astype(F32)

    xc = ctx
    for l in range(depth):
        last = l == depth - 1
        mod = mod_all[l]
        g1 = row2(g_norm1[l])
        g2 = row2(g_norm2[l])
        w5 = w_in[l][:, :5 * wb].astype(BF16)
        e_mat, w2s, a_re, a_im = _s5_prep(s5_lam_re[l], s5_lam_im[l], s5_log_dt[l], s5_b_re[l],
                                          s5_b_im[l], s5_c_re[l], s5_c_im[l], s5_d[l])
        cw = jnp.zeros((32, wb), F32).at[:CONV_WIDTH].set(conv_w[l])
        small = (s5_w_glu[l].astype(BF16), None, fnet_w[l].astype(BF16),
                 _block_diag(pool_w[l]).astype(BF16), row2(pool_scale[l]), cw, row2(conv_b[l]),
                 row2(conv_ln_g[l]), row2(conv_ln_b[l]), conv_w_out[l].astype(BF16))
        big = (w_in[l][:, 5 * wb:].astype(BF16), w_branch[l].astype(BF16), w_out[l].astype(BF16),
               mlp_w1[l].astype(BF16), mlp_w2[l].astype(BF16))

        def with_wcs(n):
            wcs = jnp.asarray(_fnet_consts(n // FFT_RADIX)[3])
            return small[:1] + (wcs,) + small[2:]

        us_c, uf_c, up_c, v_c = _pre_call(xc, None, mod, g1, w5, mod_row=ctx_row, tn=ctx_len)
        us, uf, up, v = _pre_call(x, pos_tabs if l == 0 else None, mod, g1, w5, mod_row=None, tn=tn)

        ys_c, ys = _s5_call(us_c, us, e_mat, w2s, a_re, a_im)

        z = _fnet_call(uf)
        x = _merge_call(x, pos_tabs if l == 0 else None, mod, g1, g2, row2(g_final) if last else None,
                        ys, z, up, v, with_wcs(seq), big, mod_row=None, tn=tn)
        if not last:
            z_c = _fnet_call(uf_c)
            xc = _merge_call(xc, None, mod, g1, g2, None, ys_c, z_c, up_c, v_c, with_wcs(ctx_len),
                             big, mod_row=ctx_row, tn=ctx_len)
    return x
```

```python
import functools
import math

import numpy as np
import jax
import jax.numpy as jnp
from jax import lax
from jax.experimental import pallas as pl
from jax.experimental.pallas import tpu as pltpu

F32 = jnp.float32
BF16 = jnp.bfloat16

D_MODEL = 1024
W_BRANCH = 256
N_BRANCH = 4
N_MOD = 6
D_FF = 4 * D_MODEL
S5_GROUPS = 16
S5_GROUP = 16
S5_STATE = 64
FNET_GROUP = 64
GRID_W = 64
CONV_WIDTH = 31
POOL_HALF = (1, 2, 4, 8)
EPS = 1e-6
POS_BASE = 10000.0

S5_CHUNK = 16
FFT_RADIX = 8
HALO = 16
LANES = 128
VMEM_LIMIT = 56 * 1024 * 1024


def _bdot(a, b):
    return jnp.dot(a, b, preferred_element_type=F32)


def _rms(x, g):
    return x * lax.rsqrt(jnp.mean(x * x, axis=-1, keepdims=True) + EPS) * g


def _mod_rows(mod_ref, row, n):
    return [mod_ref[pl.ds(row, 1), k * D_MODEL:(k + 1) * D_MODEL] for k in range(n)]


def _mod_kernel(c_ref, w_ref, b_ref, o_ref):
    c = c_ref[...]
    act = c * jax.nn.sigmoid(c)
    o_ref[...] = jnp.dot(act, w_ref[...], preferred_element_type=F32,
                         precision=lax.Precision.HIGHEST) + b_ref[...]


def _mod_call(cpad, w_mod, b_mod):
    depth, d, n = w_mod.shape
    tc = 1536
    return pl.pallas_call(
        _mod_kernel,
        grid=(depth, n // tc),
        in_specs=[
            pl.BlockSpec((8, d), lambda l, j: (0, 0)),
            pl.BlockSpec((None, d, tc), lambda l, j: (l, 0, j)),
            pl.BlockSpec((None, 1, tc), lambda l, j: (l, 0, j)),
        ],
        out_specs=pl.BlockSpec((None, 8, tc), lambda l, j: (l, 0, j)),
        out_shape=jax.ShapeDtypeStruct((depth, 8, n), F32),
        compiler_params=pltpu.CompilerParams(
            dimension_semantics=("parallel", "parallel"), vmem_limit_bytes=VMEM_LIMIT),
        name="mod",
    )(cpad, w_mod, b_mod.reshape(depth, 1, n))


def _pos_tile(rowtab_ref, coltile_ref, tile_idx, tn):
    rows_per_tile = tn // GRID_W
    half = D_MODEL // 2
    parts = [jnp.broadcast_to(rowtab_ref[pl.ds(tile_idx * rows_per_tile + q, 1), :], (GRID_W, half))
             for q in range(rows_per_tile)]
    return jnp.concatenate([jnp.concatenate(parts, axis=0), coltile_ref[...]], axis=1)


def _pre_kernel(*refs, has_pos, mod_row):
    if has_pos:
        (x_ref, rowtab_ref, coltile_ref, mod_ref, g1_ref, w5_ref,
         us_ref, uf_ref, up_ref, v_ref, fbuf) = refs
        x = x_ref[...] + _pos_tile(rowtab_ref, coltile_ref, pl.program_id(1), x_ref.shape[0])
    else:
        x_ref, mod_ref, g1_ref, w5_ref, us_ref, uf_ref, up_ref, v_ref, fbuf = refs
        x = x_ref[...]
    row = pl.program_id(0) if mod_row is None else mod_row
    sh, sc = _mod_rows(mod_ref, row, 2)
    h = (_rms(x, g1_ref[...]) * (1.0 + sc) + sh).astype(BF16)
    z = _bdot(h, w5_ref[...])
    wb = W_BRANCH
    tn = z.shape[0]
    us_ref[0] = z[:, 0:LANES]
    us_ref[1] = z[:, LANES:wb]
    fbuf[0] = z[:, wb:wb + LANES]
    fbuf[1] = z[:, wb + LANES:2 * wb]
    for n2 in range(FFT_RADIX):
        for hf in range(2):
            lo = n2 * wb + hf * LANES
            uf_ref[:, lo:lo + LANES] = fbuf[hf, pl.ds(n2, tn // FFT_RADIX, stride=FFT_RADIX), :].astype(BF16)
    up_ref[...] = z[:, 2 * wb:3 * wb]
    v_ref[...] = z[:, 3 * wb:4 * wb] * jax.nn.sigmoid(z[:, 4 * wb:5 * wb])


def _pre_call(x, pos_tabs, mod, g1, w5, *, mod_row, tn):
    b, s, d = x.shape
    wb = W_BRANCH
    has_pos = pos_tabs is not None
    tok = lambda bi, i: (bi, i, 0)
    const = lambda bi, i: (0, 0)
    in_specs = [pl.BlockSpec((None, tn, d), tok)]
    args = [x]
    if has_pos:
        in_specs += [pl.BlockSpec(a.shape, const) for a in pos_tabs]
        args += list(pos_tabs)
    in_specs += [
        pl.BlockSpec(mod.shape, const),
        pl.BlockSpec((1, d), const),
        pl.BlockSpec(w5.shape, const),
    ]
    args += [mod, g1, w5]
    out_spec = pl.BlockSpec((None, tn, wb), tok)
    return pl.pallas_call(
        functools.partial(_pre_kernel, has_pos=has_pos, mod_row=mod_row),
        grid=(b, s // tn),
        in_specs=in_specs,
        out_specs=[
            pl.BlockSpec((None, 2, tn, LANES), lambda bi, i: (bi, 0, i, 0)),
            pl.BlockSpec((None, tn // FFT_RADIX, FFT_RADIX * wb), tok),
            out_spec, out_spec,
        ],
        out_shape=[
            jax.ShapeDtypeStruct((b, 2, s, LANES), F32),
            jax.ShapeDtypeStruct((b, s // FFT_RADIX, FFT_RADIX * wb), BF16),
            jax.ShapeDtypeStruct((b, s, wb), F32),
            jax.ShapeDtypeStruct((b, s, wb), F32),
        ],
        scratch_shapes=[pltpu.VMEM((2, tn, LANES), F32)],
        compiler_params=pltpu.CompilerParams(
            dimension_semantics=("parallel", "parallel"), vmem_limit_bytes=VMEM_LIMIT),
        name="pre",
    )(*args)


def _gather_lane_block(pieces, sel):
    blk = lax.broadcasted_iota(jnp.int32, pieces[0].shape, 1) // S5_GROUP
    out = None
    for q, p in enumerate(pieces):
        shift = ((q - sel) % 8) * S5_GROUP
        r = pltpu.roll(p, shift, axis=1) if shift else p
        out = r if out is None else jnp.where(blk == q, r, out)
    return out


def _s5_kernel(usc_ref, us_ref, e_ref, w2_ref, ar_ref, ai_ref, yc_ref, y_ref, ubuf, sre, sim, ybuf,
               *, n_ctx_chunks):
    groups, nc, _ = ubuf.shape
    t = S5_CHUNK
    half = LANES // 2
    n_lat_chunks = nc - n_ctx_chunks
    rb = 64

    def relayout_in(src_ref, src_chunk0, nrows, dst_row0):
        for hf in range(2):
            for jj in range(2):
                pieces = [src_ref[hf, pl.ds(src_chunk0 * t + jj * 8 + m, nrows, stride=t), :]
                          for m in range(8)]
                for g8 in range(8):
                    col = _gather_lane_block(pieces, g8).astype(BF16)
                    ubuf[hf * 8 + g8, pl.ds(dst_row0, nrows), jj * LANES:(jj + 1) * LANES] = col

    relayout_in(usc_ref, 0, n_ctx_chunks, 0)

    def in_body(i, carry):
        r0 = pl.multiple_of(i * rb, rb)
        relayout_in(us_ref, r0, rb, pl.multiple_of(n_ctx_chunks + r0, 16))
        return carry

    lax.fori_loop(0, n_lat_chunks // rb, in_body, 0)

    for g in range(groups):
        e = _bdot(ubuf[g], e_ref[g])
        sre[pl.ds(g, nc, stride=groups), :] = e[:, 0:LANES]
        sim[pl.ds(g, nc, stride=groups), :] = e[:, LANES:2 * LANES]

    a_re = ar_ref[...]
    a_im = ai_ref[...]
    is_fwd = lax.broadcasted_iota(jnp.int32, (groups, LANES), 1) < half

    def step(i, carry):
        s_re, s_im = carry
        kf = i
        kr = jnp.where(i < n_ctx_chunks, n_ctx_chunks - 1 - i, nc - 1 + n_ctx_chunks - i)
        rf = pl.multiple_of(kf * groups, groups)
        rr = pl.multiple_of(kr * groups, groups)
        e_re = jnp.where(is_fwd, sre[pl.ds(rf, groups), :], sre[pl.ds(rr, groups), :])
        e_im = jnp.where(is_fwd, sim[pl.ds(rf, groups), :], sim[pl.ds(rr, groups), :])
        sre[pl.ds(rf, groups), 0:half] = s_re[:, 0:half]
        sim[pl.ds(rf, groups), 0:half] = s_im[:, 0:half]
        sre[pl.ds(rr, groups), half:LANES] = s_re[:, half:LANES]
        sim[pl.ds(rr, groups), half:LANES] = s_im[:, half:LANES]
        n_re = a_re * s_re - a_im * s_im + e_re
        n_im = a_re * s_im + a_im * s_re + e_im
        return n_re, n_im

    zero = jnp.zeros((groups, LANES), F32)
    lax.fori_loop(0, nc, step, (zero, zero))

    for hf in range(2):
        for g8 in range(8):
            g = hf * 8 + g8
            s_re = sre[pl.ds(g, nc, stride=groups), :].astype(BF16)
            s_im = sim[pl.ds(g, nc, stride=groups), :].astype(BF16)
            lhs = jnp.concatenate([ubuf[g], s_re, s_im], axis=1)
            ybuf[g8] = _bdot(lhs, w2_ref[g])

        def relayout_out(dst_ref, src_row0, nrows, dst_chunk0):
            for ii in range(2):
                pieces = [ybuf[q, pl.ds(src_row0, nrows), ii * LANES:(ii + 1) * LANES] for q in range(8)]
                for m in range(8):
                    tok = _gather_lane_block(pieces, m)
                    dst_ref[hf, pl.ds(dst_chunk0 * t + ii * 8 + m, nrows, stride=t), :] = tok

        relayout_out(yc_ref, 0, n_ctx_chunks, 0)

        def out_body(i, carry):
            r0 = pl.multiple_of(i * rb, rb)
            relayout_out(y_ref, pl.multiple_of(n_ctx_chunks + r0, 8), rb, r0)
            return carry

        lax.fori_loop(0, n_lat_chunks // rb, out_body, 0)


def _s5_call(us_c, us, e_mat, w2, a_re, a_im):
    b, _, lc, _ = us_c.shape
    s = us.shape[2]
    t = S5_CHUNK
    n_ctx_chunks = lc // t
    nc = n_ctx_chunks + s // t
    groups = S5_GROUPS
    lw = t * S5_GROUP
    single = pl.Buffered(1)
    slab = lambda n: pl.BlockSpec((None, 2, n, LANES), lambda bi: (bi, 0, 0, 0))
    slab1 = lambda n: pl.BlockSpec((None, 2, n, LANES), lambda bi: (bi, 0, 0, 0), pipeline_mode=single)
    full3 = lambda a: pl.BlockSpec(a.shape, lambda bi: (0, 0, 0), pipeline_mode=single)
    full2 = lambda a: pl.BlockSpec(a.shape, lambda bi: (0, 0))
    return pl.pallas_call(
        functools.partial(_s5_kernel, n_ctx_chunks=n_ctx_chunks),
        grid=(b,),
        in_specs=[slab1(lc), slab1(s), full3(e_mat), full3(w2), full2(a_re), full2(a_im)],
        out_specs=[slab(lc), slab(s)],
        out_shape=[jax.ShapeDtypeStruct((b, 2, lc, LANES), F32),
                   jax.ShapeDtypeStruct((b, 2, s, LANES), F32)],
        scratch_shapes=[pltpu.VMEM((groups, nc, lw), BF16),
                        pltpu.VMEM((nc * groups, LANES), F32),
                        pltpu.VMEM((nc * groups, LANES), F32),
                        pltpu.VMEM((8, nc, lw), F32)],
        compiler_params=pltpu.CompilerParams(
            dimension_semantics=("parallel",), vmem_limit_bytes=VMEM_LIMIT),
        name="s5",
    )(us_c, us, e_mat, w2, a_re, a_im)


def _s5prep_kernel(prow_ref, pcol_ref, bt_ref, ct_ref, dcol_ref, e_ref, w2_ref, al_ref):
    t, cg, p = S5_CHUNK, S5_GROUP, S5_STATE
    hi = lax.Precision.HIGHEST
    lw = t * cg

    def discretise(lam_re, lam_im, log_dt):
        dt = jnp.exp(log_dt)
        mag = jnp.exp(lam_re * dt)
        ang = lam_im * dt
        a_re = mag * jnp.cos(ang)
        a_im = mag * jnp.sin(ang)
        den = lam_re * lam_re + lam_im * lam_im
        f_re = ((a_re - 1) * lam_re + a_im * lam_im) / den
        f_im = (a_im * lam_re - (a_re - 1) * lam_im) / den
        return a_re, a_im, f_re, f_im

    def powers(a_re, a_im):
        out = [(jnp.ones_like(a_re), jnp.zeros_like(a_im))]
        for _ in range(t):
            pr, pi = out[-1]
            out.append((pr * a_re - pi * a_im, pr * a_im + pi * a_re))
        return out

    a_re, a_im, f_re, f_im = discretise(prow_ref[0:1, :], prow_ref[1:2, :], prow_ref[2:3, :])
    pw = powers(a_re, a_im)
    bt_re = bt_ref[0]
    bt_im = bt_ref[1]
    bb_re = f_re * bt_re - f_im * bt_im
    bb_im = f_re * bt_im + f_im * bt_re
    fwd_lane = lax.broadcasted_iota(jnp.int32, (1, LANES), 1) < p
    for j in range(t):
        s_re = jnp.where(fwd_lane, pw[t - 1 - j][0], pw[j][0])
        s_im = jnp.where(fwd_lane, pw[t - 1 - j][1], pw[j][1])
        e_ref[j * cg:(j + 1) * cg, 0:LANES] = (s_re * bb_re - s_im * bb_im).astype(e_ref.dtype)
        e_ref[j * cg:(j + 1) * cg, LANES:2 * LANES] = (s_re * bb_im + s_im * bb_re).astype(e_ref.dtype)
    al_ref[0:1, :] = pw[t][0]
    al_ref[1:2, :] = pw[t][1]

    c_re, c_im, _, _ = discretise(pcol_ref[:, 0:1], pcol_ref[:, 1:2], pcol_ref[:, 2:3])
    pwc = powers(c_re, c_im)
    nb = 2 * t
    wide = nb * cg
    blk = lax.broadcasted_iota(jnp.int32, (2 * p, wide), 1) // cg
    fwd_row = lax.broadcasted_iota(jnp.int32, (2 * p, wide), 0) < p
    zero_col = jnp.zeros((2 * p, 1), F32)
    pw_re = jnp.zeros((2 * p, wide), F32)
    pw_im = jnp.zeros((2 * p, wide), F32)
    for b in range(nb):
        tf, tr = b - (t - 1), t - b
        fr, fi = pwc[tf] if 0 <= tf <= t else (zero_col, zero_col)
        rr, ri = pwc[tr] if 0 <= tr <= t else (zero_col, zero_col)
        pw_re = jnp.where(blk == b, jnp.where(fwd_row, fr, rr), pw_re)
        pw_im = jnp.where(blk == b, jnp.where(fwd_row, fi, ri), pw_im)
    ct_re = jnp.concatenate([ct_ref[0]] * (wide // LANES), axis=1)
    ct_im = jnp.concatenate([ct_ref[1]] * (wide // LANES), axis=1)
    rw_re = ct_re * pw_re - ct_im * pw_im
    rw_im = ct_re * pw_im + ct_im * pw_re
    od = w2_ref.dtype
    w2_ref[lw:lw + p, :] = rw_re[0:p, lw:2 * lw].astype(od)
    w2_ref[lw + p:lw + 2 * p, :] = rw_re[p:2 * p, 0:lw].astype(od)
    w2_ref[lw + 2 * p:lw + 3 * p, :] = (-rw_im[0:p, lw:2 * lw]).astype(od)
    w2_ref[lw + 3 * p:lw + 4 * p, :] = (-rw_im[p:2 * p, 0:lw]).astype(od)

    lane_f = lax.broadcasted_iota(jnp.int32, (cg, LANES), 1) < p

    def kern(sel):
        br = jnp.where(sel, bb_re, 0.0)
        bi = jnp.where(sel, bb_im, 0.0)
        return (jnp.dot(br, rw_re, preferred_element_type=F32, precision=hi)
                - jnp.dot(bi, rw_im, preferred_element_type=F32, precision=hi))

    kwf = kern(lane_f)
    kwr = kern(jnp.logical_not(lane_f))
    dcol = jnp.concatenate([dcol_ref[...]] * (lw // LANES), axis=1)
    li = lax.broadcasted_iota(jnp.int32, (cg, lw), 1)
    ri = lax.broadcasted_iota(jnp.int32, (cg, lw), 0)
    for j in range(t):
        mf = pltpu.roll(kwf, (wide - (t - 1 - j) * cg) % wide, axis=1)[:, 0:lw]
        mr = pltpu.roll(kwr, (wide - (t - j) * cg) % wide, axis=1)[:, 0:lw]
        m = mf + mr + jnp.where(li == j * cg + ri, dcol, 0.0)
        w2_ref[j * cg:(j + 1) * cg, :] = m.astype(od)


def _s5_prep_call(lam_re, lam_im, log_dt, b_re, b_im, c_re, c_im, d):
    depth = lam_re.shape[0]
    g, p, cg, t = S5_GROUPS, S5_STATE, S5_GROUP, S5_CHUNK
    both = lambda a: jnp.concatenate([a[:, 0], a[:, 1]], axis=-1)
    rows = lambda a: jnp.concatenate([a[:, 0], a[:, 1]], axis=-2)
    swap = lambda a: a.transpose(0, 1, 2, 4, 3)
    prow = jnp.stack([both(lam_re), both(lam_im),
                      both(jnp.broadcast_to(log_dt[..., None], lam_re.shape))], axis=2).astype(F32)
    pcol = jnp.concatenate([prow.transpose(0, 1, 3, 2), jnp.zeros((depth, g, 2 * p, 5), F32)], axis=-1)
    bt = jnp.stack([both(swap(b_re)), both(swap(b_im))], axis=2).astype(F32)
    ct = jnp.stack([rows(swap(c_re)), rows(swap(c_im))], axis=2).astype(F32)
    ct = jnp.tile(ct, (1, 1, 1, 1, LANES // cg))
    dcol = jnp.broadcast_to(d.astype(F32).reshape(depth, g, cg, 1), (depth, g, cg, LANES))
    lw = t * cg
    spec = lambda a: pl.BlockSpec((None, None) + a.shape[2:], lambda l, gi: (l, gi) + (0,) * (a.ndim - 2))
    e_mat, w2, alpha = pl.pallas_call(
        _s5prep_kernel,
        grid=(depth, g),
        in_specs=[spec(prow), spec(pcol), spec(bt), spec(ct), spec(dcol)],
        out_specs=[pl.BlockSpec((None, None, lw, lw), lambda l, gi: (l, gi, 0, 0)),
                   pl.BlockSpec((None, None, 2 * lw, lw), lambda l, gi: (l, gi, 0, 0)),
                   pl.BlockSpec((None, None, 2, LANES), lambda l, gi: (l, gi, 0, 0))],
        out_shape=[jax.ShapeDtypeStruct((depth, g, lw, lw), BF16),
                   jax.ShapeDtypeStruct((depth, g, 2 * lw, lw), BF16),
                   jax.ShapeDtypeStruct((depth, g, 2, LANES), F32)],
        compiler_params=pltpu.CompilerParams(
            dimension_semantics=("parallel", "parallel"), vmem_limit_bytes=VMEM_LIMIT),
        name="s5prep",
    )(prow, pcol, bt, ct, dcol)
    return e_mat, w2, alpha[:, :, 0], alpha[:, :, 1]


def _cmul_const(xr, xi, c, s):
    def close(a, b):
        return abs(a - b) < 1e-12
    if close(c, 1) and close(s, 0):
        return xr, xi
    if close(c, -1) and close(s, 0):
        return -xr, -xi
    if close(c, 0) and close(s, 1):
        return -xi, xr
    if close(c, 0) and close(s, -1):
        return xi, -xr
    return c * xr - s * xi, c * xi + s * xr


def _fft_list(xs):
    n = len(xs)
    if n == 1:
        return xs
    ev = _fft_list(xs[0::2])
    od = _fft_list(xs[1::2])
    out = [None] * n
    for k in range(n // 2):
        ang = -2.0 * math.pi * k / n
        tr, ti = _cmul_const(od[k][0], od[k][1], math.cos(ang), math.sin(ang))
        out[k] = (ev[k][0] + tr, ev[k][1] + ti)
        out[k + n // 2] = (ev[k][0] - tr, ev[k][1] - ti)
    return out


def _fnet_kernel(x_ref, f_ref, tc_ref, ts_ref, o_ref):
    wb = W_BRANCH
    x = x_ref[...]
    a_re = _bdot(f_ref[0].astype(BF16), x)
    a_im = _bdot(f_ref[1].astype(BF16), x)
    tc = tc_ref[...]
    ts = ts_ref[...]
    b_re = a_re * tc + a_im * ts
    b_im = a_im * tc - a_re * ts
    xs = [(b_re[:, n * wb:(n + 1) * wb], b_im[:, n * wb:(n + 1) * wb]) for n in range(FFT_RADIX)]
    zs = _fft_list(xs)
    for k2 in range(FFT_RADIX):
        o_ref[k2, :, 0:wb] = zs[k2][0].astype(o_ref.dtype)
        o_ref[k2, :, wb:2 * wb] = zs[k2][1].astype(o_ref.dtype)


@functools.lru_cache(maxsize=None)
def _fnet_consts(n1):
    n = n1 * FFT_RADIX
    k = np.arange(n1, dtype=np.int64)
    ang1 = 2.0 * np.pi * ((k[:, None] * k[None, :]) % n1) / n1
    f = np.stack([np.cos(ang1), -np.sin(ang1)]).astype(np.float32)
    n2 = np.arange(FFT_RADIX, dtype=np.int64)
    ang2 = 2.0 * np.pi * ((k[:, None] * n2[None, :]) % n) / n
    tc = np.repeat(np.cos(ang2), W_BRANCH, axis=1).astype(np.float32)
    ts = np.repeat(np.sin(ang2), W_BRANCH, axis=1).astype(np.float32)
    c = np.arange(FNET_GROUP, dtype=np.int64)
    angc = 2.0 * np.pi * ((c[:, None] * c[None, :]) % FNET_GROUP) / FNET_GROUP
    norm = 1.0 / math.sqrt(n * FNET_GROUP)
    ng = W_BRANCH // FNET_GROUP
    wcs = np.zeros((2 * W_BRANCH, W_BRANCH), np.float32)
    for gi in range(ng):
        sl = slice(gi * FNET_GROUP, (gi + 1) * FNET_GROUP)
        wcs[sl, sl] = np.cos(angc) * norm
        wcs[W_BRANCH + gi * FNET_GROUP:W_BRANCH + (gi + 1) * FNET_GROUP, sl] = np.sin(angc) * norm
    return f, tc, ts, wcs


def _fnet_call(x):
    b, n1, _ = x.shape
    wb = W_BRANCH
    l = n1 * FFT_RADIX
    tk = min(n1, 256)
    f, tc, ts, _ = _fnet_consts(n1)
    out = pl.pallas_call(
        _fnet_kernel,
        grid=(b, n1 // tk),
        in_specs=[
            pl.BlockSpec((None, n1, FFT_RADIX * wb), lambda bi, i: (bi, 0, 0)),
            pl.BlockSpec((2, tk, n1), lambda bi, i: (0, i, 0)),
            pl.BlockSpec((tk, FFT_RADIX * wb), lambda bi, i: (i, 0)),
            pl.BlockSpec((tk, FFT_RADIX * wb), lambda bi, i: (i, 0)),
        ],
        out_specs=pl.BlockSpec((None, FFT_RADIX, tk, 2 * wb), lambda bi, i: (bi, 0, i, 0)),
        out_shape=jax.ShapeDtypeStruct((b, FFT_RADIX, n1, 2 * wb), BF16),
        compiler_params=pltpu.CompilerParams(
            dimension_semantics=("parallel", "parallel"), vmem_limit_bytes=VMEM_LIMIT),
        name="fnet",
    )(x, jnp.asarray(f), jnp.asarray(tc), jnp.asarray(ts))
    return out.reshape(b, l, 2 * wb)


def _merge_kernel(*refs, seq_len, tn, has_pos, final, mod_row):
    refs = list(refs)
    x_ref = refs.pop(0)
    rowtab_ref, coltile_ref = (refs.pop(0), refs.pop(0)) if has_pos else (None, None)
    mod_ref, g1_ref, g2_ref = refs[0:3]
    refs = refs[3:]
    gf_ref = refs.pop(0) if final else None
    (ys_ref, z_ref, up_prev, up_cur, up_next, v_prev, v_cur, v_next,
     wglu_ref, wcs_ref, wfn_ref, wpool_ref, pscale_ref, cw_ref, cb_ref, lng_ref, lnb_ref, wcv_ref,
     wgate_ref, wbr_ref, wout_ref, w1_ref, w2_ref, o_ref, pbuf, vbuf) = refs

    d = D_MODEL
    i = pl.program_id(1)
    nt = pl.num_programs(1)
    x = x_ref[...]
    if has_pos:
        x = x + _pos_tile(rowtab_ref, coltile_ref, i, tn)
    row = pl.program_id(0) if mod_row is None else mod_row
    sh1, sc1, ga1, sh2, sc2, ga2 = _mod_rows(mod_ref, row, N_MOD)
    h = (_rms(x, g1_ref[...]) * (1.0 + sc1) + sh1).astype(BF16)

    def gated_proj(k, branch):
        gate = jax.nn.sigmoid(_bdot(h, wgate_ref[:, k * d:(k + 1) * d]))
        return gate * _bdot(branch.astype(BF16), wbr_ref[k])

    y = jax.nn.gelu(jnp.concatenate([ys_ref[0], ys_ref[1]], axis=1))
    merged = gated_proj(0, y * jax.nn.sigmoid(_bdot(y.astype(BF16), wglu_ref[...])))

    yf = _bdot(z_ref[...], wcs_ref[...].astype(BF16))
    merged += gated_proj(1, _bdot(yf.astype(BF16), wfn_ref[...]))

    first = i == 0
    last = i == nt - 1
    for buf, prev, cur, nxt in ((pbuf, up_prev, up_cur, up_next), (vbuf, v_prev, v_cur, v_next)):
        buf[0:HALO, :] = jnp.where(first, 0.0, prev[...])
        buf[HALO:HALO + tn, :] = cur[...]
        buf[HALO + tn:2 * HALO + tn, :] = jnp.where(last, 0.0, nxt[...])

    t_pos = i * tn + lax.broadcasted_iota(jnp.int32, (tn, LANES), 0)
    lane = lax.broadcasted_iota(jnp.int32, (tn, LANES), 1)
    low = lane < LANES // 2

    def count(half):
        return (jnp.minimum(t_pos + half, seq_len) - jnp.maximum(t_pos - half, 0)).astype(F32)

    def window(col, offsets):
        acc = None
        for off in offsets:
            part = pbuf[pl.ds(HALO + off, tn), col * LANES:(col + 1) * LANES]
            acc = part if acc is None else acc + part
        return acc

    pooled = []
    for col in range(2):
        h_small, h_big = POOL_HALF[2 * col], POOL_HALF[2 * col + 1]
        s_small = window(col, range(-h_small, h_small))
        s_big = s_small + window(col, list(range(-h_big, -h_small)) + list(range(h_small, h_big)))
        mean = jnp.where(low, s_small / count(h_small), s_big / count(h_big))
        pooled.append(mean - up_cur[:, col * LANES:(col + 1) * LANES])
    pooled = jnp.concatenate(pooled, axis=1).astype(BF16)
    merged += gated_proj(2, _bdot(pooled, wpool_ref[...]) * pscale_ref[...])

    acc = None
    for k in range(CONV_WIDTH):
        term = cw_ref[pl.ds(k, 1), :] * vbuf[pl.ds(HALO - CONV_WIDTH // 2 + k, tn), :]
        acc = term if acc is None else acc + term
    acc = acc + cb_ref[...]
    mu = jnp.mean(acc, axis=-1, keepdims=True)
    xc = acc - mu
    yn = xc * lax.rsqrt(jnp.mean(xc * xc, axis=-1, keepdims=True) + EPS) * lng_ref[...] + lnb_ref[...]
    yn = yn * jax.nn.sigmoid(yn)
    merged += gated_proj(3, _bdot(yn.astype(BF16), wcv_ref[...]))

    x1 = x + ga1 * _bdot(merged.astype(BF16), wout_ref[...])

    h2 = (_rms(x1, g2_ref[...]) * (1.0 + sc2) + sh2).astype(BF16)
    acc2 = None
    for cidx in range(D_FF // d):
        a = jnp.maximum(_bdot(h2, w1_ref[:, cidx * d:(cidx + 1) * d]), 0.0)
        part = _bdot((a * a).astype(BF16), w2_ref[cidx * d:(cidx + 1) * d, :])
        acc2 = part if acc2 is None else acc2 + part
    x2 = x1 + ga2 * acc2
    if final:
        x2 = _rms(x2, gf_ref[...])
    o_ref[...] = x2


def _merge_call(x, pos_tabs, mod, g1, g2, gf, ys, z, up, v, small, big, *, mod_row, tn):
    b, s, d = x.shape
    wb = W_BRANCH
    has_pos = pos_tabs is not None
    final = gf is not None
    nh = tn // HALO
    tok = lambda bi, i: (bi, i, 0)
    prev = lambda bi, i: (bi, jnp.maximum(i * nh - 1, 0), 0)
    nxt = lambda bi, i: (bi, jnp.minimum((i + 1) * nh, s // HALO - 1), 0)

    def const_spec(a):
        nd = a.ndim
        return pl.BlockSpec(a.shape, lambda bi, i: (0,) * nd, pipeline_mode=pl.Buffered(1))

    in_specs = [pl.BlockSpec((None, tn, d), tok)]
    args = [x]
    for a in (tuple(pos_tabs) if has_pos else ()) + (mod, g1, g2) + ((gf,) if final else ()):
        in_specs.append(const_spec(a))
        args.append(a)
    in_specs += [pl.BlockSpec((None, 2, tn, LANES), lambda bi, i: (bi, 0, i, 0)),
                 pl.BlockSpec((None, tn, 2 * wb), tok)]
    args += [ys, z]
    for a in (up, v):
        in_specs += [pl.BlockSpec((None, HALO, wb), prev), pl.BlockSpec((None, tn, wb), tok),
                     pl.BlockSpec((None, HALO, wb), nxt)]
        args += [a, a, a]
    for a in tuple(small) + tuple(big):
        in_specs.append(const_spec(a))
        args.append(a)

    return pl.pallas_call(
        functools.partial(_merge_kernel, seq_len=s, tn=tn, has_pos=has_pos, final=final,
                          mod_row=mod_row),
        grid=(b, s // tn),
        in_specs=in_specs,
        out_specs=pl.BlockSpec((None, tn, d), tok),
        out_shape=jax.ShapeDtypeStruct((b, s, d), F32),
        scratch_shapes=[pltpu.VMEM((tn + 2 * HALO, wb), F32), pltpu.VMEM((tn + 2 * HALO, wb), F32)],
        compiler_params=pltpu.CompilerParams(
            dimension_semantics=("parallel", "parallel"), vmem_limit_bytes=VMEM_LIMIT),
        name="merge",
    )(*args)


def _pos_tables(rows, tn):
    quarter = D_MODEL // 4
    freq = 1.0 / (POS_BASE ** (jnp.arange(quarter, dtype=F32) / quarter))

    def enc(p):
        ang = p.astype(F32)[:, None] * freq[None, :]
        return jnp.concatenate([jnp.sin(ang), jnp.cos(ang)], axis=-1)

    return enc(jnp.arange(rows)), enc(jnp.tile(jnp.arange(GRID_W), tn // GRID_W))


def _block_diag(w):
    g, a, b = w.shape
    out = jnp.zeros((g * a, g * b), w.dtype)
    for k in range(g):
        out = out.at[k * a:(k + 1) * a, k * b:(k + 1) * b].set(w[k])
    return out


def kernel(x, c, ctx, c_ctx, w_mod, b_mod, g_norm1, w_in, s5_lam_re, s5_lam_im, s5_log_dt, s5_b_re, s5_b_im, s5_c_re, s5_c_im, s5_d, s5_w_glu, fnet_w, pool_w, pool_scale, conv_w, conv_b, conv_ln_g, conv_ln_b, conv_w_out, w_branch, w_out, g_norm2, mlp_w1, mlp_w2, g_final):
    bsz, seq, d = x.shape
    ctx_len = ctx.shape[1]
    depth = w_mod.shape[0]
    wb = W_BRANCH
    assert bsz < 8 and seq % 512 == 0 and ctx_len % (S5_CHUNK * 8) == 0

    tn = 512
    pos_tabs = _pos_tables(seq // GRID_W, tn)
    cpad = jnp.zeros((8, d), F32).at[:bsz].set(c).at[bsz].set(c_ctx)
    mod_all = _mod_call(cpad, w_mod, b_mod)
    ctx_row = bsz
    e_all, w2_all, are_all, aim_all = _s5_prep_call(s5_lam_re, s5_lam_im, s5_log_dt, s5_b_re, s5_b_im,
                                                    s5_c_re, s5_c_im, s5_d)
    row2 = lambda a: a.reshape(1, -1).astype(F32)

    xc = ctx
    for l in range(depth):
        last = l == depth - 1
        mod = mod_all[l]
        g1 = row2(g_norm1[l])
        g2 = row2(g_norm2[l])
        w5 = w_in[l][:, :5 * wb].astype(BF16)
        cw = jnp.zeros((32, wb), F32).at[:CONV_WIDTH].set(conv_w[l])
        small = (s5_w_glu[l].astype(BF16), None, fnet_w[l].astype(BF16),
                 _block_diag(pool_w[l]).astype(BF16), row2(pool_scale[l]), cw, row2(conv_b[l]),
                 row2(conv_ln_g[l]), row2(conv_ln_b[l]), conv_w_out[l].astype(BF16))
        big = (w_in[l][:, 5 * wb:].astype(BF16), w_branch[l].astype(BF16), w_out[l].astype(BF16),
               mlp_w1[l].astype(BF16), mlp_w2[l].astype(BF16))

        def with_wcs(n):
            wcs = jnp.asarray(_fnet_consts(n // FFT_RADIX)[3])
            return small[:1] + (wcs,) + small[2:]

        us_c, uf_c, up_c, v_c = _pre_call(xc, None, mod, g1, w5, mod_row=ctx_row, tn=ctx_len)
        us, uf, up, v = _pre_call(x, pos_tabs if l == 0 else None, mod, g1, w5, mod_row=None, tn=tn)

        ys_c, ys = _s5_call(us_c, us, e_all[l], w2_all[l], are_all[l], aim_all[l])

        z = _fnet_call(uf)
        x = _merge_call(x, pos_tabs if l == 0 else None, mod, g1, g2, row2(g_final) if last else None,
                        ys, z, up, v, with_wcs(seq), big, mod_row=None, tn=tn)
        if not last:
            z_c = _fnet_call(uf_c)
            xc = _merge_call(xc, None, mod, g1, g2, None, ys_c, z_c, up_c, v_c, with_wcs(ctx_len),
                             big, mod_row=ctx_row, tn=ctx_len)
    return x
```

```python
import functools
import math

import numpy as np
import jax
import jax.numpy as jnp
from jax import lax
from jax.experimental import pallas as pl
from jax.experimental.pallas import tpu as pltpu

F32 = jnp.float32
BF16 = jnp.bfloat16

D_MODEL = 1024
W_BRANCH = 256
N_BRANCH = 4
N_MOD = 6
D_FF = 4 * D_MODEL
S5_GROUPS = 16
S5_GROUP = 16
S5_STATE = 64
FNET_GROUP = 64
GRID_W = 64
CONV_WIDTH = 31
POOL_HALF = (1, 2, 4, 8)
EPS = 1e-6
POS_BASE = 10000.0

S5_CHUNK = 16
S5_STATE_PITCH = 24
FFT_RADIX = 8
HALO = 16
LANES = 128
SUBLANES = 8
VMEM_LIMIT = 56 * 1024 * 1024


def _bdot(a, b):
    return jnp.dot(a, b, preferred_element_type=F32)


def _sigmoid(x):
    return 0.5 * jnp.tanh(0.5 * x) + 0.5


def _rms(x, g):
    return x * lax.rsqrt(jnp.mean(x * x, axis=-1, keepdims=True) + EPS) * g


def _mod_rows(mod_ref, row, n):
    return [mod_ref[pl.ds(row, 1), k * D_MODEL:(k + 1) * D_MODEL] for k in range(n)]


def _mod_kernel(c_ref, w_ref, b_ref, o_ref):
    c = c_ref[...]
    act = c * jax.nn.sigmoid(c)
    o_ref[...] = jnp.dot(act, w_ref[...], preferred_element_type=F32,
                         precision=lax.Precision.HIGHEST) + b_ref[...]


def _mod_call(cpad, w_mod, b_mod):
    depth, d, n = w_mod.shape
    tc = 1536
    return pl.pallas_call(
        _mod_kernel,
        grid=(depth, n // tc),
        in_specs=[
            pl.BlockSpec((8, d), lambda l, j: (0, 0)),
            pl.BlockSpec((None, d, tc), lambda l, j: (l, 0, j)),
            pl.BlockSpec((None, 1, tc), lambda l, j: (l, 0, j)),
        ],
        out_specs=pl.BlockSpec((None, 8, tc), lambda l, j: (l, 0, j)),
        out_shape=jax.ShapeDtypeStruct((depth, 8, n), F32),
        compiler_params=pltpu.CompilerParams(
            dimension_semantics=("parallel", "parallel"), vmem_limit_bytes=VMEM_LIMIT),
        name="mod",
    )(cpad, w_mod, b_mod.reshape(depth, 1, n))


def _pos_tile(rowtab_ref, coltile_ref, tile_idx, tn):
    rows_per_tile = tn // GRID_W
    half = D_MODEL // 2
    parts = [jnp.broadcast_to(rowtab_ref[pl.ds(tile_idx * rows_per_tile + q, 1), :], (GRID_W, half))
             for q in range(rows_per_tile)]
    return jnp.concatenate([jnp.concatenate(parts, axis=0), coltile_ref[...]], axis=1)


def _pre_kernel(*refs, has_pos, mod_row):
    if has_pos:
        (x_ref, rowtab_ref, coltile_ref, mod_ref, g1_ref, w5_ref,
         us_ref, uf_ref, up_ref, v_ref, fbuf) = refs
        x = x_ref[...] + _pos_tile(rowtab_ref, coltile_ref, pl.program_id(1), x_ref.shape[0])
    else:
        x_ref, mod_ref, g1_ref, w5_ref, us_ref, uf_ref, up_ref, v_ref, fbuf = refs
        x = x_ref[...]
    row = pl.program_id(0) if mod_row is None else mod_row
    sh, sc = _mod_rows(mod_ref, row, 2)
    h = (_rms(x, g1_ref[...]) * (1.0 + sc) + sh).astype(BF16)
    z = _bdot(h, w5_ref[...])
    wb = W_BRANCH
    tn = z.shape[0]
    us_ref[0] = z[:, 0:LANES]
    us_ref[1] = z[:, LANES:wb]
    fbuf[0] = z[:, wb:wb + LANES]
    fbuf[1] = z[:, wb + LANES:2 * wb]
    for n2 in range(FFT_RADIX):
        for hf in range(2):
            lo = n2 * wb + hf * LANES
            uf_ref[:, lo:lo + LANES] = fbuf[hf, pl.ds(n2, tn // FFT_RADIX, stride=FFT_RADIX), :].astype(BF16)
    up_ref[...] = z[:, 2 * wb:3 * wb]
    v_ref[...] = z[:, 3 * wb:4 * wb] * jax.nn.sigmoid(z[:, 4 * wb:5 * wb])


def _pre_call(x, pos_tabs, mod, g1, w5, *, mod_row, tn):
    b, s, d = x.shape
    wb = W_BRANCH
    has_pos = pos_tabs is not None
    tok = lambda bi, i: (bi, i, 0)
    const = lambda bi, i: (0, 0)
    in_specs = [pl.BlockSpec((None, tn, d), tok)]
    args = [x]
    if has_pos:
        in_specs += [pl.BlockSpec(a.shape, const) for a in pos_tabs]
        args += list(pos_tabs)
    in_specs += [
        pl.BlockSpec(mod.shape, const),
        pl.BlockSpec((1, d), const),
        pl.BlockSpec(w5.shape, const),
    ]
    args += [mod, g1, w5]
    out_spec = pl.BlockSpec((None, tn, wb), tok)
    return pl.pallas_call(
        functools.partial(_pre_kernel, has_pos=has_pos, mod_row=mod_row),
        grid=(b, s // tn),
        in_specs=in_specs,
        out_specs=[
            pl.BlockSpec((None, 2, tn, LANES), lambda bi, i: (bi, 0, i, 0)),
            pl.BlockSpec((None, tn // FFT_RADIX, FFT_RADIX * wb), tok),
            out_spec, out_spec,
        ],
        out_shape=[
            jax.ShapeDtypeStruct((b, 2, s, LANES), F32),
            jax.ShapeDtypeStruct((b, s // FFT_RADIX, FFT_RADIX * wb), BF16),
            jax.ShapeDtypeStruct((b, s, wb), F32),
            jax.ShapeDtypeStruct((b, s, wb), F32),
        ],
        scratch_shapes=[pltpu.VMEM((2, tn, LANES), F32)],
        compiler_params=pltpu.CompilerParams(
            dimension_semantics=("parallel", "parallel"), vmem_limit_bytes=VMEM_LIMIT),
        name="pre",
    )(*args)


def _transpose_lane_blocks(pieces):
    xs = list(pieces)
    blk = lax.broadcasted_iota(jnp.int32, xs[0].shape, 1) // S5_GROUP
    for dist in (4, 2, 1):
        upper = (blk & dist) != 0
        for i in range(8):
            if i & dist:
                continue
            a, b = xs[i], xs[i + dist]
            xs[i] = jnp.where(upper, pltpu.roll(b, dist * S5_GROUP, axis=1), a)
            xs[i + dist] = jnp.where(upper, b, pltpu.roll(a, LANES - dist * S5_GROUP, axis=1))
    return xs


def _s5_kernel(usc_ref, us_ref, e_ref, w2_ref, ar_ref, ai_ref, yc_ref, y_ref, ubuf, sre, sim, ybuf,
               *, n_ctx_chunks):
    groups, nc, _ = ubuf.shape
    t = S5_CHUNK
    half = LANES // 2
    n_lat_chunks = nc - n_ctx_chunks
    rb = 64
    pitch = S5_STATE_PITCH

    def relayout_in(src_ref, src_chunk0, nrows, dst_row0):
        for hf in range(2):
            for jj in range(2):
                pieces = [src_ref[hf, pl.ds(src_chunk0 * t + jj * 8 + m, nrows, stride=t), :]
                          for m in range(8)]
                for g8, col in enumerate(_transpose_lane_blocks(pieces)):
                    ubuf[hf * 8 + g8, pl.ds(dst_row0, nrows), jj * LANES:(jj + 1) * LANES] = col.astype(BF16)

    relayout_in(usc_ref, 0, n_ctx_chunks, 0)

    def in_body(i, carry):
        r0 = pl.multiple_of(i * rb, rb)
        relayout_in(us_ref, r0, rb, pl.multiple_of(n_ctx_chunks + r0, 16))
        return carry

    lax.fori_loop(0, n_lat_chunks // rb, in_body, 0)

    for g in range(groups):
        e = _bdot(ubuf[g], e_ref[g])
        sre[pl.ds(g, nc, stride=pitch), :] = e[:, 0:LANES]
        sim[pl.ds(g, nc, stride=pitch), :] = e[:, LANES:2 * LANES]

    a_re = ar_ref[...]
    a_im = ai_ref[...]
    is_fwd = lax.broadcasted_iota(jnp.int32, (groups, LANES), 1) < half

    def step(i, carry):
        s_re, s_im = carry
        kf = i
        kr = jnp.where(i < n_ctx_chunks, n_ctx_chunks - 1 - i, nc - 1 + n_ctx_chunks - i)
        rf = pl.multiple_of(kf * pitch, SUBLANES)
        rr = pl.multiple_of(kr * pitch, SUBLANES)
        e_re = jnp.where(is_fwd, sre[pl.ds(rf, groups), :], sre[pl.ds(rr, groups), :])
        e_im = jnp.where(is_fwd, sim[pl.ds(rf, groups), :], sim[pl.ds(rr, groups), :])
        sre[pl.ds(rf, groups), 0:half] = s_re[:, 0:half]
        sim[pl.ds(rf, groups), 0:half] = s_im[:, 0:half]
        sre[pl.ds(rr, groups), half:LANES] = s_re[:, half:LANES]
        sim[pl.ds(rr, groups), half:LANES] = s_im[:, half:LANES]
        n_re = a_re * s_re - a_im * s_im + e_re
        n_im = a_re * s_im + a_im * s_re + e_im
        return n_re, n_im

    zero = jnp.zeros((groups, LANES), F32)
    lax.fori_loop(0, nc, step, (zero, zero))

    for hf in range(2):
        for g8 in range(8):
            g = hf * 8 + g8
            s_re = sre[pl.ds(g, nc, stride=pitch), :].astype(BF16)
            s_im = sim[pl.ds(g, nc, stride=pitch), :].astype(BF16)
            lhs = jnp.concatenate([ubuf[g], s_re, s_im], axis=1)
            ybuf[g8] = _bdot(lhs, w2_ref[g])

        def relayout_out(dst_ref, src_row0, nrows, dst_chunk0):
            for ii in range(2):
                pieces = [ybuf[q, pl.ds(src_row0, nrows), ii * LANES:(ii + 1) * LANES] for q in range(8)]
                for m, tok in enumerate(_transpose_lane_blocks(pieces)):
                    dst_ref[hf, pl.ds(dst_chunk0 * t + ii * 8 + m, nrows, stride=t), :] = tok

        relayout_out(yc_ref, 0, n_ctx_chunks, 0)

        def out_body(i, carry):
            r0 = pl.multiple_of(i * rb, rb)
            relayout_out(y_ref, pl.multiple_of(n_ctx_chunks + r0, 8), rb, r0)
            return carry

        lax.fori_loop(0, n_lat_chunks // rb, out_body, 0)


def _s5_call(us_c, us, e_mat, w2, a_re, a_im):
    b, _, lc, _ = us_c.shape
    s = us.shape[2]
    t = S5_CHUNK
    n_ctx_chunks = lc // t
    nc = n_ctx_chunks + s // t
    groups = S5_GROUPS
    lw = t * S5_GROUP
    single = pl.Buffered(1)
    slab = lambda n: pl.BlockSpec((None, 2, n, LANES), lambda bi: (bi, 0, 0, 0))
    slab1 = lambda n: pl.BlockSpec((None, 2, n, LANES), lambda bi: (bi, 0, 0, 0), pipeline_mode=single)
    full3 = lambda a: pl.BlockSpec(a.shape, lambda bi: (0, 0, 0), pipeline_mode=single)
    full2 = lambda a: pl.BlockSpec(a.shape, lambda bi: (0, 0))
    return pl.pallas_call(
        functools.partial(_s5_kernel, n_ctx_chunks=n_ctx_chunks),
        grid=(b,),
        in_specs=[slab1(lc), slab1(s), full3(e_mat), full3(w2), full2(a_re), full2(a_im)],
        out_specs=[slab(lc), slab(s)],
        out_shape=[jax.ShapeDtypeStruct((b, 2, lc, LANES), F32),
                   jax.ShapeDtypeStruct((b, 2, s, LANES), F32)],
        scratch_shapes=[pltpu.VMEM((groups, nc, lw), BF16),
                        pltpu.VMEM((nc * S5_STATE_PITCH, LANES), F32),
                        pltpu.VMEM((nc * S5_STATE_PITCH, LANES), F32),
                        pltpu.VMEM((8, nc, lw), F32)],
        compiler_params=pltpu.CompilerParams(
            dimension_semantics=("parallel",), vmem_limit_bytes=VMEM_LIMIT),
        name="s5",
    )(us_c, us, e_mat, w2, a_re, a_im)


def _s5prep_kernel(prow_ref, pcol_ref, bt_ref, ct_ref, dcol_ref, e_ref, w2_ref, al_ref):
    t, cg, p = S5_CHUNK, S5_GROUP, S5_STATE
    hi = lax.Precision.HIGHEST
    lw = t * cg

    def discretise(lam_re, lam_im, log_dt):
        dt = jnp.exp(log_dt)
        mag = jnp.exp(lam_re * dt)
        ang = lam_im * dt
        a_re = mag * jnp.cos(ang)
        a_im = mag * jnp.sin(ang)
        den = lam_re * lam_re + lam_im * lam_im
        f_re = ((a_re - 1) * lam_re + a_im * lam_im) / den
        f_im = (a_im * lam_re - (a_re - 1) * lam_im) / den
        return a_re, a_im, f_re, f_im

    def powers(a_re, a_im):
        out = [(jnp.ones_like(a_re), jnp.zeros_like(a_im))]
        for _ in range(t):
            pr, pi = out[-1]
            out.append((pr * a_re - pi * a_im, pr * a_im + pi * a_re))
        return out

    a_re, a_im, f_re, f_im = discretise(prow_ref[0:1, :], prow_ref[1:2, :], prow_ref[2:3, :])
    pw = powers(a_re, a_im)
    bt_re = bt_ref[0]
    bt_im = bt_ref[1]
    bb_re = f_re * bt_re - f_im * bt_im
    bb_im = f_re * bt_im + f_im * bt_re
    fwd_lane = lax.broadcasted_iota(jnp.int32, (1, LANES), 1) < p
    for j in range(t):
        s_re = jnp.where(fwd_lane, pw[t - 1 - j][0], pw[j][0])
        s_im = jnp.where(fwd_lane, pw[t - 1 - j][1], pw[j][1])
        e_ref[j * cg:(j + 1) * cg, 0:LANES] = (s_re * bb_re - s_im * bb_im).astype(e_ref.dtype)
        e_ref[j * cg:(j + 1) * cg, LANES:2 * LANES] = (s_re * bb_im + s_im * bb_re).astype(e_ref.dtype)
    al_ref[0:1, :] = pw[t][0]
    al_ref[1:2, :] = pw[t][1]

    c_re, c_im, _, _ = discretise(pcol_ref[:, 0:1], pcol_ref[:, 1:2], pcol_ref[:, 2:3])
    pwc = powers(c_re, c_im)
    nb = 2 * t
    wide = nb * cg
    blk = lax.broadcasted_iota(jnp.int32, (2 * p, wide), 1) // cg
    fwd_row = lax.broadcasted_iota(jnp.int32, (2 * p, wide), 0) < p
    zero_col = jnp.zeros((2 * p, 1), F32)
    pw_re = jnp.zeros((2 * p, wide), F32)
    pw_im = jnp.zeros((2 * p, wide), F32)
    for b in range(nb):
        tf, tr = b - (t - 1), t - b
        fr, fi = pwc[tf] if 0 <= tf <= t else (zero_col, zero_col)
        rr, ri = pwc[tr] if 0 <= tr <= t else (zero_col, zero_col)
        pw_re = jnp.where(blk == b, jnp.where(fwd_row, fr, rr), pw_re)
        pw_im = jnp.where(blk == b, jnp.where(fwd_row, fi, ri), pw_im)
    ct_re = jnp.concatenate([ct_ref[0]] * (wide // LANES), axis=1)
    ct_im = jnp.concatenate([ct_ref[1]] * (wide // LANES), axis=1)
    rw_re = ct_re * pw_re - ct_im * pw_im
    rw_im = ct_re * pw_im + ct_im * pw_re
    od = w2_ref.dtype
    w2_ref[lw:lw + p, :] = rw_re[0:p, lw:2 * lw].astype(od)
    w2_ref[lw + p:lw + 2 * p, :] = rw_re[p:2 * p, 0:lw].astype(od)
    w2_ref[lw + 2 * p:lw + 3 * p, :] = (-rw_im[0:p, lw:2 * lw]).astype(od)
    w2_ref[lw + 3 * p:lw + 4 * p, :] = (-rw_im[p:2 * p, 0:lw]).astype(od)

    lane_f = lax.broadcasted_iota(jnp.int32, (cg, LANES), 1) < p

    def kern(sel):
        br = jnp.where(sel, bb_re, 0.0)
        bi = jnp.where(sel, bb_im, 0.0)
        return (jnp.dot(br, rw_re, preferred_element_type=F32, precision=hi)
                - jnp.dot(bi, rw_im, preferred_element_type=F32, precision=hi))

    kwf = kern(lane_f)
    kwr = kern(jnp.logical_not(lane_f))
    dcol = jnp.concatenate([dcol_ref[...]] * (lw // LANES), axis=1)
    li = lax.broadcasted_iota(jnp.int32, (cg, lw), 1)
    ri = lax.broadcasted_iota(jnp.int32, (cg, lw), 0)
    for j in range(t):
        mf = pltpu.roll(kwf, (wide - (t - 1 - j) * cg) % wide, axis=1)[:, 0:lw]
        mr = pltpu.roll(kwr, (wide - (t - j) * cg) % wide, axis=1)[:, 0:lw]
        m = mf + mr + jnp.where(li == j * cg + ri, dcol, 0.0)
        w2_ref[j * cg:(j + 1) * cg, :] = m.astype(od)


def _s5_prep_call(lam_re, lam_im, log_dt, b_re, b_im, c_re, c_im, d):
    depth = lam_re.shape[0]
    g, p, cg, t = S5_GROUPS, S5_STATE, S5_GROUP, S5_CHUNK
    both = lambda a: jnp.concatenate([a[:, 0], a[:, 1]], axis=-1)
    rows = lambda a: jnp.concatenate([a[:, 0], a[:, 1]], axis=-2)
    swap = lambda a: a.transpose(0, 1, 2, 4, 3)
    prow = jnp.stack([both(lam_re), both(lam_im),
                      both(jnp.broadcast_to(log_dt[..., None], lam_re.shape))], axis=2).astype(F32)
    pcol = jnp.concatenate([prow.transpose(0, 1, 3, 2), jnp.zeros((depth, g, 2 * p, 5), F32)], axis=-1)
    bt = jnp.stack([both(swap(b_re)), both(swap(b_im))], axis=2).astype(F32)
    ct = jnp.stack([rows(swap(c_re)), rows(swap(c_im))], axis=2).astype(F32)
    ct = jnp.tile(ct, (1, 1, 1, 1, LANES // cg))
    dcol = jnp.broadcast_to(d.astype(F32).reshape(depth, g, cg, 1), (depth, g, cg, LANES))
    lw = t * cg
    spec = lambda a: pl.BlockSpec((None, None) + a.shape[2:], lambda l, gi: (l, gi) + (0,) * (a.ndim - 2))
    e_mat, w2, alpha = pl.pallas_call(
        _s5prep_kernel,
        grid=(depth, g),
        in_specs=[spec(prow), spec(pcol), spec(bt), spec(ct), spec(dcol)],
        out_specs=[pl.BlockSpec((None, None, lw, lw), lambda l, gi: (l, gi, 0, 0)),
                   pl.BlockSpec((None, None, 2 * lw, lw), lambda l, gi: (l, gi, 0, 0)),
                   pl.BlockSpec((None, None, 2, LANES), lambda l, gi: (l, gi, 0, 0))],
        out_shape=[jax.ShapeDtypeStruct((depth, g, lw, lw), BF16),
                   jax.ShapeDtypeStruct((depth, g, 2 * lw, lw), BF16),
                   jax.ShapeDtypeStruct((depth, g, 2, LANES), F32)],
        compiler_params=pltpu.CompilerParams(
            dimension_semantics=("parallel", "parallel"), vmem_limit_bytes=VMEM_LIMIT),
        name="s5prep",
    )(prow, pcol, bt, ct, dcol)
    return e_mat, w2, alpha[:, :, 0], alpha[:, :, 1]


def _cmul_const(xr, xi, c, s):
    def close(a, b):
        return abs(a - b) < 1e-12
    if close(c, 1) and close(s, 0):
        return xr, xi
    if close(c, -1) and close(s, 0):
        return -xr, -xi
    if close(c, 0) and close(s, 1):
        return -xi, xr
    if close(c, 0) and close(s, -1):
        return xi, -xr
    return c * xr - s * xi, c * xi + s * xr


def _fft_list(xs):
    n = len(xs)
    if n == 1:
        return xs
    ev = _fft_list(xs[0::2])
    od = _fft_list(xs[1::2])
    out = [None] * n
    for k in range(n // 2):
        ang = -2.0 * math.pi * k / n
        tr, ti = _cmul_const(od[k][0], od[k][1], math.cos(ang), math.sin(ang))
        out[k] = (ev[k][0] + tr, ev[k][1] + ti)
        out[k + n // 2] = (ev[k][0] - tr, ev[k][1] - ti)
    return out


def _fnet_kernel(x_ref, f_ref, tc_ref, ts_ref, o_ref):
    wb = W_BRANCH
    x = x_ref[...]
    a_re = _bdot(f_ref[0].astype(BF16), x)
    a_im = _bdot(f_ref[1].astype(BF16), x)
    tc = tc_ref[...]
    ts = ts_ref[...]
    b_re = a_re * tc + a_im * ts
    b_im = a_im * tc - a_re * ts
    xs = [(b_re[:, n * wb:(n + 1) * wb], b_im[:, n * wb:(n + 1) * wb]) for n in range(FFT_RADIX)]
    zs = _fft_list(xs)
    for k2 in range(FFT_RADIX):
        o_ref[k2, :, 0:wb] = zs[k2][0].astype(o_ref.dtype)
        o_ref[k2, :, wb:2 * wb] = zs[k2][1].astype(o_ref.dtype)


@functools.lru_cache(maxsize=None)
def _fnet_consts(n1):
    n = n1 * FFT_RADIX
    k = np.arange(n1, dtype=np.int64)
    ang1 = 2.0 * np.pi * ((k[:, None] * k[None, :]) % n1) / n1
    f = np.stack([np.cos(ang1), -np.sin(ang1)]).astype(np.float32)
    n2 = np.arange(FFT_RADIX, dtype=np.int64)
    ang2 = 2.0 * np.pi * ((k[:, None] * n2[None, :]) % n) / n
    tc = np.repeat(np.cos(ang2), W_BRANCH, axis=1).astype(np.float32)
    ts = np.repeat(np.sin(ang2), W_BRANCH, axis=1).astype(np.float32)
    c = np.arange(FNET_GROUP, dtype=np.int64)
    angc = 2.0 * np.pi * ((c[:, None] * c[None, :]) % FNET_GROUP) / FNET_GROUP
    norm = 1.0 / math.sqrt(n * FNET_GROUP)
    ng = W_BRANCH // FNET_GROUP
    wcs = np.zeros((2 * W_BRANCH, W_BRANCH), np.float32)
    for gi in range(ng):
        sl = slice(gi * FNET_GROUP, (gi + 1) * FNET_GROUP)
        wcs[sl, sl] = np.cos(angc) * norm
        wcs[W_BRANCH + gi * FNET_GROUP:W_BRANCH + (gi + 1) * FNET_GROUP, sl] = np.sin(angc) * norm
    return f, tc, ts, wcs


def _fnet_call(x):
    b, n1, _ = x.shape
    wb = W_BRANCH
    l = n1 * FFT_RADIX
    tk = min(n1, 256)
    f, tc, ts, _ = _fnet_consts(n1)
    out = pl.pallas_call(
        _fnet_kernel,
        grid=(b, n1 // tk),
        in_specs=[
            pl.BlockSpec((None, n1, FFT_RADIX * wb), lambda bi, i: (bi, 0, 0)),
            pl.BlockSpec((2, tk, n1), lambda bi, i: (0, i, 0)),
            pl.BlockSpec((tk, FFT_RADIX * wb), lambda bi, i: (i, 0)),
            pl.BlockSpec((tk, FFT_RADIX * wb), lambda bi, i: (i, 0)),
        ],
        out_specs=pl.BlockSpec((None, FFT_RADIX, tk, 2 * wb), lambda bi, i: (bi, 0, i, 0)),
        out_shape=jax.ShapeDtypeStruct((b, FFT_RADIX, n1, 2 * wb), BF16),
        compiler_params=pltpu.CompilerParams(
            dimension_semantics=("parallel", "parallel"), vmem_limit_bytes=VMEM_LIMIT),
        name="fnet",
    )(x, jnp.asarray(f), jnp.asarray(tc), jnp.asarray(ts))
    return out.reshape(b, l, 2 * wb)


def _merge_kernel(*refs, seq_len, tn, has_pos, final, mod_row):
    refs = list(refs)
    x_ref = refs.pop(0)
    rowtab_ref, coltile_ref = (refs.pop(0), refs.pop(0)) if has_pos else (None, None)
    mod_ref, g1_ref, g2_ref = refs[0:3]
    refs = refs[3:]
    gf_ref = refs.pop(0) if final else None
    (ys_ref, z_ref, up_prev, up_cur, up_next, v_prev, v_cur, v_next,
     wglu_ref, wcs_ref, wfn_ref, wpool_ref, pscale_ref, cw_ref, cb_ref, lng_ref, lnb_ref, wcv_ref,
     wgate_ref, wbr_ref, wout_ref, w1_ref, w2_ref, o_ref, pbuf, vbuf, sbuf) = refs

    d = D_MODEL
    i = pl.program_id(1)
    nt = pl.num_programs(1)
    x = x_ref[...]
    if has_pos:
        x = x + _pos_tile(rowtab_ref, coltile_ref, i, tn)
    row = pl.program_id(0) if mod_row is None else mod_row
    sh1, sc1, ga1, sh2, sc2, ga2 = _mod_rows(mod_ref, row, N_MOD)
    h = (_rms(x, g1_ref[...]) * (1.0 + sc1) + sh1).astype(BF16)

    first = i == 0
    last = i == nt - 1
    for buf, prev, cur, nxt in ((pbuf, up_prev, up_cur, up_next), (vbuf, v_prev, v_cur, v_next)):
        buf[0:HALO, :] = jnp.where(first, 0.0, prev[...])
        buf[HALO:HALO + tn, :] = cur[...]
        buf[HALO + tn:2 * HALO + tn, :] = jnp.where(last, 0.0, nxt[...])

    def gate_logits(k):
        return _sigmoid(_bdot(h, wgate_ref[:, k * d:(k + 1) * d])).astype(BF16)

    logits = []
    conv_cols = []
    span = tn + 2 * HALO - SUBLANES
    for cl in range(W_BRANCH // LANES):
        logits.append(gate_logits(cl))
        lanes = slice(cl * LANES, (cl + 1) * LANES)
        for s in range(SUBLANES):
            sbuf[s] = vbuf[pl.ds(s, span), lanes]
        acc = None
        for k in range(CONV_WIDTH):
            q, s = divmod(HALO - CONV_WIDTH // 2 + k, SUBLANES)
            term = cw_ref[pl.ds(k, 1), lanes] * sbuf[s, pl.ds(q * SUBLANES, tn), :]
            acc = term if acc is None else acc + term
        conv_cols.append(acc)
    logits.append(gate_logits(2))
    acc = jnp.concatenate(conv_cols, axis=1) + cb_ref[...]
    mu = jnp.mean(acc, axis=-1, keepdims=True)
    xc = acc - mu
    yn = xc * lax.rsqrt(jnp.mean(xc * xc, axis=-1, keepdims=True) + EPS) * lng_ref[...] + lnb_ref[...]
    yn = yn * _sigmoid(yn)
    b_conv = _bdot(yn.astype(BF16), wcv_ref[...])

    t_pos = i * tn + lax.broadcasted_iota(jnp.int32, (tn, LANES), 0)
    lane = lax.broadcasted_iota(jnp.int32, (tn, LANES), 1)
    low = lane < LANES // 2

    def count(half):
        return (jnp.minimum(t_pos + half, seq_len) - jnp.maximum(t_pos - half, 0)).astype(F32)

    def window(col, offsets):
        acc = None
        for off in offsets:
            part = pbuf[pl.ds(HALO + off, tn), col * LANES:(col + 1) * LANES]
            acc = part if acc is None else acc + part
        return acc

    pooled = []
    for col in range(2):
        h_small, h_big = POOL_HALF[2 * col], POOL_HALF[2 * col + 1]
        s_small = window(col, range(-h_small, h_small))
        s_big = s_small + window(col, list(range(-h_big, -h_small)) + list(range(h_small, h_big)))
        mean = jnp.where(low, s_small / count(h_small), s_big / count(h_big))
        pooled.append(mean - up_cur[:, col * LANES:(col + 1) * LANES])
    pooled = jnp.concatenate(pooled, axis=1).astype(BF16)
    b_pool = _bdot(pooled, wpool_ref[...]) * pscale_ref[...]
    logits.append(gate_logits(3))

    y = jax.nn.gelu(jnp.concatenate([ys_ref[0], ys_ref[1]], axis=1))
    b_s5 = y * _sigmoid(_bdot(y.astype(BF16), wglu_ref[...]))

    yf = _bdot(z_ref[...], wcs_ref[...].astype(BF16))
    b_fnet = _bdot(yf.astype(BF16), wfn_ref[...])

    merged = None
    for k, branch in enumerate((b_s5, b_fnet, b_pool, b_conv)):
        part = logits[k] * _bdot(branch.astype(BF16), wbr_ref[k])
        merged = part if merged is None else merged + part

    x1 = x + ga1 * _bdot(merged.astype(BF16), wout_ref[...])

    h2 = (_rms(x1, g2_ref[...]) * (1.0 + sc2) + sh2).astype(BF16)
    acc2 = None
    for cidx in range(D_FF // d):
        a = jnp.maximum(_bdot(h2, w1_ref[:, cidx * d:(cidx + 1) * d]), 0.0)
        part = _bdot((a * a).astype(BF16), w2_ref[cidx * d:(cidx + 1) * d, :])
        acc2 = part if acc2 is None else acc2 + part
    x2 = x1 + ga2 * acc2
    if final:
        x2 = _rms(x2, gf_ref[...])
    o_ref[...] = x2


def _merge_call(x, pos_tabs, mod, g1, g2, gf, ys, z, up, v, small, big, *, mod_row, tn):
    b, s, d = x.shape
    wb = W_BRANCH
    has_pos = pos_tabs is not None
    final = gf is not None
    nh = tn // HALO
    tok = lambda bi, i: (bi, i, 0)
    prev = lambda bi, i: (bi, jnp.maximum(i * nh - 1, 0), 0)
    nxt = lambda bi, i: (bi, jnp.minimum((i + 1) * nh, s // HALO - 1), 0)

    def const_spec(a):
        nd = a.ndim
        return pl.BlockSpec(a.shape, lambda bi, i: (0,) * nd, pipeline_mode=pl.Buffered(1))

    in_specs = [pl.BlockSpec((None, tn, d), tok)]
    args = [x]
    for a in (tuple(pos_tabs) if has_pos else ()) + (mod, g1, g2) + ((gf,) if final else ()):
        in_specs.append(const_spec(a))
        args.append(a)
    in_specs += [pl.BlockSpec((None, 2, tn, LANES), lambda bi, i: (bi, 0, i, 0)),
                 pl.BlockSpec((None, tn, 2 * wb), tok)]
    args += [ys, z]
    for a in (up, v):
        in_specs += [pl.BlockSpec((None, HALO, wb), prev), pl.BlockSpec((None, tn, wb), tok),
                     pl.BlockSpec((None, HALO, wb), nxt)]
        args += [a, a, a]
    for a in tuple(small) + tuple(big):
        in_specs.append(const_spec(a))
        args.append(a)

    return pl.pallas_call(
        functools.partial(_merge_kernel, seq_len=s, tn=tn, has_pos=has_pos, final=final,
                          mod_row=mod_row),
        grid=(b, s // tn),
        in_specs=in_specs,
        out_specs=pl.BlockSpec((None, tn, d), tok),
        out_shape=jax.ShapeDtypeStruct((b, s, d), F32),
        scratch_shapes=[pltpu.VMEM((tn + 2 * HALO, wb), F32), pltpu.VMEM((tn + 2 * HALO, wb), F32),
                        pltpu.VMEM((SUBLANES, tn + 2 * HALO - SUBLANES, LANES), F32)],
        compiler_params=pltpu.CompilerParams(
            dimension_semantics=("parallel", "parallel"), vmem_limit_bytes=VMEM_LIMIT),
        name="merge",
    )(*args)


def _pos_tables(rows, tn):
    quarter = D_MODEL // 4
    freq = 1.0 / (POS_BASE ** (jnp.arange(quarter, dtype=F32) / quarter))

    def enc(p):
        ang = p.astype(F32)[:, None] * freq[None, :]
        return jnp.concatenate([jnp.sin(ang), jnp.cos(ang)], axis=-1)

    return enc(jnp.arange(rows)), enc(jnp.tile(jnp.arange(GRID_W), tn // GRID_W))


def _block_diag(w):
    g, a, b = w.shape
    out = jnp.zeros((g * a, g * b), w.dtype)
    for k in range(g):
        out = out.at[k * a:(k + 1) * a, k * b:(k + 1) * b].set(w[k])
    return out


def kernel(x, c, ctx, c_ctx, w_mod, b_mod, g_norm1, w_in, s5_lam_re, s5_lam_im, s5_log_dt, s5_b_re, s5_b_im, s5_c_re, s5_c_im, s5_d, s5_w_glu, fnet_w, pool_w, pool_scale, conv_w, conv_b, conv_ln_g, conv_ln_b, conv_w_out, w_branch, w_out, g_norm2, mlp_w1, mlp_w2, g_final):
    bsz, seq, d = x.shape
    ctx_len = ctx.shape[1]
    depth = w_mod.shape[0]
    wb = W_BRANCH
    assert bsz < 8 and seq % 512 == 0 and ctx_len % (S5_CHUNK * 8) == 0

    tn = 512
    pos_tabs = _pos_tables(seq // GRID_W, tn)
    cpad = jnp.zeros((8, d), F32).at[:bsz].set(c).at[bsz].set(c_ctx)
    mod_all = _mod_call(cpad, w_mod, b_mod)
    ctx_row = bsz
    e_all, w2_all, are_all, aim_all = _s5_prep_call(s5_lam_re, s5_lam_im, s5_log_dt, s5_b_re, s5_b_im,
                                                    s5_c_re, s5_c_im, s5_d)
    row2 = lambda a: a.reshape(1, -1).astype(F32)

    xc = ctx
    for l in range(depth):
        last = l == depth - 1
        mod = mod_all[l]
        g1 = row2(g_norm1[l])
        g2 = row2(g_norm2[l])
        w5 = w_in[l][:, :5 * wb].astype(BF16)
        cw = jnp.zeros((32, wb), F32).at[:CONV_WIDTH].set(conv_w[l])
        small = (s5_w_glu[l].astype(BF16), None, fnet_w[l].astype(BF16),
                 _block_diag(pool_w[l]).astype(BF16), row2(pool_scale[l]), cw, row2(conv_b[l]),
                 row2(conv_ln_g[l]), row2(conv_ln_b[l]), conv_w_out[l].astype(BF16))
        big = (w_in[l][:, 5 * wb:].astype(BF16), w_branch[l].astype(BF16), w_out[l].astype(BF16),
               mlp_w1[l].astype(BF16), mlp_w2[l].astype(BF16))

        def with_wcs(n):
            wcs = jnp.asarray(_fnet_consts(n // FFT_RADIX)[3])
            return small[:1] + (wcs,) + small[2:]

        us_c, uf_c, up_c, v_c = _pre_call(xc, None, mod, g1, w5, mod_row=ctx_row, tn=ctx_len)
        us, uf, up, v = _pre_call(x, pos_tabs if l == 0 else None, mod, g1, w5, mod_row=None, tn=tn)

        ys_c, ys = _s5_call(us_c, us, e_all[l], w2_all[l], are_all[l], aim_all[l])

        z = _fnet_call(uf)
        x = _merge_call(x, pos_tabs if l == 0 else None, mod, g1, g2, row2(g_final) if last else None,
                        ys, z, up, v, with_wcs(seq), big, mod_row=None, tn=tn)
        if not last:
            z_c = _fnet_call(uf_c)
            xc = _merge_call(xc, None, mod, g1, g2, None, ys_c, z_c, up_c, v_c, with_wcs(ctx_len),
                             big, mod_row=ctx_row, tn=ctx_len)
    return x
```

```python
import functools
import math

import numpy as np
import jax
import jax.numpy as jnp
from jax import lax
from jax.experimental import pallas as pl
from jax.experimental.pallas import tpu as pltpu

F32 = jnp.float32
BF16 = jnp.bfloat16

D_MODEL = 1024
W_BRANCH = 256
N_BRANCH = 4
N_MOD = 6
D_FF = 4 * D_MODEL
S5_GROUPS = 16
S5_GROUP = 16
S5_STATE = 64
FNET_GROUP = 64
GRID_W = 64
CONV_WIDTH = 31
POOL_HALF = (1, 2, 4, 8)
EPS = 1e-6
POS_BASE = 10000.0

S5_CHUNK = 16
S5_STATE_PITCH = 24
FFT_RADIX = 8
HALO = 16
LANES = 128
SUBLANES = 8
VMEM_LIMIT = 56 * 1024 * 1024


def _bdot(a, b):
    return jnp.dot(a, b, preferred_element_type=F32)


def _sigmoid(x):
    return 0.5 * jnp.tanh(0.5 * x) + 0.5


def _rms(x, g):
    return x * lax.rsqrt(jnp.mean(x * x, axis=-1, keepdims=True) + EPS) * g


def _param_spec(a, layer, single=False):
    mode = dict(pipeline_mode=pl.Buffered(1)) if single else {}
    if layer is None:
        return pl.BlockSpec(a.shape, lambda *_: (0,) * a.ndim, **mode)
    return pl.BlockSpec((None,) + a.shape[1:], lambda *_: (layer,) + (0,) * (a.ndim - 1), **mode)


def _mod_rows(mod_ref, row, n):
    return [mod_ref[pl.ds(row, 1), k * D_MODEL:(k + 1) * D_MODEL] for k in range(n)]


def _mod_kernel(c_ref, w_ref, b_ref, o_ref):
    c = c_ref[...]
    act = c * jax.nn.sigmoid(c)
    o_ref[...] = jnp.dot(act, w_ref[...], preferred_element_type=F32,
                         precision=lax.Precision.HIGHEST) + b_ref[...]


def _mod_call(cpad, w_mod, b_mod):
    depth, d, n = w_mod.shape
    tc = 1536
    return pl.pallas_call(
        _mod_kernel,
        grid=(depth, n // tc),
        in_specs=[
            pl.BlockSpec((8, d), lambda l, j: (0, 0)),
            pl.BlockSpec((None, d, tc), lambda l, j: (l, 0, j)),
            pl.BlockSpec((None, 1, tc), lambda l, j: (l, 0, j)),
        ],
        out_specs=pl.BlockSpec((None, 8, tc), lambda l, j: (l, 0, j)),
        out_shape=jax.ShapeDtypeStruct((depth, 8, n), F32),
        compiler_params=pltpu.CompilerParams(
            dimension_semantics=("parallel", "parallel"), vmem_limit_bytes=VMEM_LIMIT),
        name="mod",
    )(cpad, w_mod, b_mod.reshape(depth, 1, n))


def _pos_tile(rowtab_ref, coltile_ref, tile_idx, tn):
    rows_per_tile = tn // GRID_W
    half = D_MODEL // 2
    parts = [jnp.broadcast_to(rowtab_ref[pl.ds(tile_idx * rows_per_tile + q, 1), :], (GRID_W, half))
             for q in range(rows_per_tile)]
    return jnp.concatenate([jnp.concatenate(parts, axis=0), coltile_ref[...]], axis=1)


def _pre_kernel(*refs, has_pos, mod_row):
    if has_pos:
        (x_ref, rowtab_ref, coltile_ref, mod_ref, g1_ref, w5_ref,
         us_ref, uf_ref, up_ref, v_ref, fbuf) = refs
        x = x_ref[...] + _pos_tile(rowtab_ref, coltile_ref, pl.program_id(1), x_ref.shape[0])
    else:
        x_ref, mod_ref, g1_ref, w5_ref, us_ref, uf_ref, up_ref, v_ref, fbuf = refs
        x = x_ref[...]
    row = pl.program_id(0) if mod_row is None else mod_row
    sh, sc = _mod_rows(mod_ref, row, 2)
    h = (_rms(x, g1_ref[...]) * (1.0 + sc) + sh).astype(BF16)
    z = _bdot(h, w5_ref[...])
    wb = W_BRANCH
    tn = z.shape[0]
    us_ref[0] = z[:, 0:LANES]
    us_ref[1] = z[:, LANES:wb]
    fbuf[0] = z[:, wb:wb + LANES]
    fbuf[1] = z[:, wb + LANES:2 * wb]
    for n2 in range(FFT_RADIX):
        for hf in range(2):
            lo = n2 * wb + hf * LANES
            uf_ref[:, lo:lo + LANES] = fbuf[hf, pl.ds(n2, tn // FFT_RADIX, stride=FFT_RADIX), :].astype(BF16)
    up_ref[...] = z[:, 2 * wb:3 * wb]
    v_ref[...] = z[:, 3 * wb:4 * wb] * jax.nn.sigmoid(z[:, 4 * wb:5 * wb])


def _pre_call(x, pos_tabs, mod, g1, w5, *, layer, mod_row, tn):
    b, s, d = x.shape
    wb = W_BRANCH
    has_pos = pos_tabs is not None
    tok = lambda bi, i: (bi, i, 0)
    in_specs = [pl.BlockSpec((None, tn, d), tok)]
    args = [x]
    if has_pos:
        in_specs += [_param_spec(a, None) for a in pos_tabs]
        args += list(pos_tabs)
    in_specs += [_param_spec(a, layer) for a in (mod, g1, w5)]
    args += [mod, g1, w5]
    out_spec = pl.BlockSpec((None, tn, wb), tok)
    return pl.pallas_call(
        functools.partial(_pre_kernel, has_pos=has_pos, mod_row=mod_row),
        grid=(b, s // tn),
        in_specs=in_specs,
        out_specs=[
            pl.BlockSpec((None, 2, tn, LANES), lambda bi, i: (bi, 0, i, 0)),
            pl.BlockSpec((None, tn // FFT_RADIX, FFT_RADIX * wb), tok),
            out_spec, out_spec,
        ],
        out_shape=[
            jax.ShapeDtypeStruct((b, 2, s, LANES), F32),
            jax.ShapeDtypeStruct((b, s // FFT_RADIX, FFT_RADIX * wb), BF16),
            jax.ShapeDtypeStruct((b, s, wb), F32),
            jax.ShapeDtypeStruct((b, s, wb), F32),
        ],
        scratch_shapes=[pltpu.VMEM((2, tn, LANES), F32)],
        compiler_params=pltpu.CompilerParams(
            dimension_semantics=("parallel", "parallel"), vmem_limit_bytes=VMEM_LIMIT),
        name="pre",
    )(*args)


def _transpose_lane_blocks(pieces):
    xs = list(pieces)
    blk = lax.broadcasted_iota(jnp.int32, xs[0].shape, 1) // S5_GROUP
    for dist in (4, 2, 1):
        upper = (blk & dist) != 0
        for i in range(8):
            if i & dist:
                continue
            a, b = xs[i], xs[i + dist]
            xs[i] = jnp.where(upper, pltpu.roll(b, dist * S5_GROUP, axis=1), a)
            xs[i + dist] = jnp.where(upper, b, pltpu.roll(a, LANES - dist * S5_GROUP, axis=1))
    return xs


def _s5_kernel(usc_ref, us_ref, e_ref, w2_ref, ar_ref, ai_ref, yc_ref, y_ref, ubuf, sre, sim, ybuf,
               *, n_ctx_chunks):
    groups, nc, _ = ubuf.shape
    t = S5_CHUNK
    half = LANES // 2
    n_lat_chunks = nc - n_ctx_chunks
    rb = 64
    pitch = S5_STATE_PITCH

    def relayout_in(src_ref, src_chunk0, nrows, dst_row0):
        for hf in range(2):
            for jj in range(2):
                pieces = [src_ref[hf, pl.ds(src_chunk0 * t + jj * 8 + m, nrows, stride=t), :]
                          for m in range(8)]
                for g8, col in enumerate(_transpose_lane_blocks(pieces)):
                    ubuf[hf * 8 + g8, pl.ds(dst_row0, nrows), jj * LANES:(jj + 1) * LANES] = col.astype(BF16)

    relayout_in(usc_ref, 0, n_ctx_chunks, 0)

    def in_body(i, carry):
        r0 = pl.multiple_of(i * rb, rb)
        relayout_in(us_ref, r0, rb, pl.multiple_of(n_ctx_chunks + r0, 16))
        return carry

    lax.fori_loop(0, n_lat_chunks // rb, in_body, 0)

    for g in range(groups):
        e = _bdot(ubuf[g], e_ref[g])
        sre[pl.ds(g, nc, stride=pitch), :] = e[:, 0:LANES]
        sim[pl.ds(g, nc, stride=pitch), :] = e[:, LANES:2 * LANES]

    a_re = ar_ref[...]
    a_im = ai_ref[...]
    is_fwd = lax.broadcasted_iota(jnp.int32, (groups, LANES), 1) < half

    def step(i, carry):
        s_re, s_im = carry
        kf = i
        kr = jnp.where(i < n_ctx_chunks, n_ctx_chunks - 1 - i, nc - 1 + n_ctx_chunks - i)
        rf = pl.multiple_of(kf * pitch, SUBLANES)
        rr = pl.multiple_of(kr * pitch, SUBLANES)
        e_re = jnp.where(is_fwd, sre[pl.ds(rf, groups), :], sre[pl.ds(rr, groups), :])
        e_im = jnp.where(is_fwd, sim[pl.ds(rf, groups), :], sim[pl.ds(rr, groups), :])
        sre[pl.ds(rf, groups), 0:half] = s_re[:, 0:half]
        sim[pl.ds(rf, groups), 0:half] = s_im[:, 0:half]
        sre[pl.ds(rr, groups), half:LANES] = s_re[:, half:LANES]
        sim[pl.ds(rr, groups), half:LANES] = s_im[:, half:LANES]
        n_re = a_re * s_re - a_im * s_im + e_re
        n_im = a_re * s_im + a_im * s_re + e_im
        return n_re, n_im

    zero = jnp.zeros((groups, LANES), F32)
    lax.fori_loop(0, nc, step, (zero, zero))

    for hf in range(2):
        for g8 in range(8):
            g = hf * 8 + g8
            s_re = sre[pl.ds(g, nc, stride=pitch), :].astype(BF16)
            s_im = sim[pl.ds(g, nc, stride=pitch), :].astype(BF16)
            lhs = jnp.concatenate([ubuf[g], s_re, s_im], axis=1)
            ybuf[g8] = _bdot(lhs, w2_ref[g])

        def relayout_out(dst_ref, src_row0, nrows, dst_chunk0):
            for ii in range(2):
                pieces = [ybuf[q, pl.ds(src_row0, nrows), ii * LANES:(ii + 1) * LANES] for q in range(8)]
                for m, tok in enumerate(_transpose_lane_blocks(pieces)):
                    dst_ref[hf, pl.ds(dst_chunk0 * t + ii * 8 + m, nrows, stride=t), :] = tok

        relayout_out(yc_ref, 0, n_ctx_chunks, 0)

        def out_body(i, carry):
            r0 = pl.multiple_of(i * rb, rb)
            relayout_out(y_ref, pl.multiple_of(n_ctx_chunks + r0, 8), rb, r0)
            return carry

        lax.fori_loop(0, n_lat_chunks // rb, out_body, 0)


def _s5_call(us_c, us, e_mat, w2, a_re, a_im, *, layer):
    b, _, lc, _ = us_c.shape
    s = us.shape[2]
    t = S5_CHUNK
    n_ctx_chunks = lc // t
    nc = n_ctx_chunks + s // t
    groups = S5_GROUPS
    lw = t * S5_GROUP
    single = pl.Buffered(1)
    slab = lambda n: pl.BlockSpec((None, 2, n, LANES), lambda bi: (bi, 0, 0, 0))
    slab1 = lambda n: pl.BlockSpec((None, 2, n, LANES), lambda bi: (bi, 0, 0, 0), pipeline_mode=single)
    return pl.pallas_call(
        functools.partial(_s5_kernel, n_ctx_chunks=n_ctx_chunks),
        grid=(b,),
        in_specs=[slab1(lc), slab1(s), _param_spec(e_mat, layer, single=True),
                  _param_spec(w2, layer, single=True), _param_spec(a_re, layer), _param_spec(a_im, layer)],
        out_specs=[slab(lc), slab(s)],
        out_shape=[jax.ShapeDtypeStruct((b, 2, lc, LANES), F32),
                   jax.ShapeDtypeStruct((b, 2, s, LANES), F32)],
        scratch_shapes=[pltpu.VMEM((groups, nc, lw), BF16),
                        pltpu.VMEM((nc * S5_STATE_PITCH, LANES), F32),
                        pltpu.VMEM((nc * S5_STATE_PITCH, LANES), F32),
                        pltpu.VMEM((8, nc, lw), F32)],
        compiler_params=pltpu.CompilerParams(
            dimension_semantics=("parallel",), vmem_limit_bytes=VMEM_LIMIT),
        name="s5",
    )(us_c, us, e_mat, w2, a_re, a_im)


def _s5prep_kernel(prow_ref, pcol_ref, bt_ref, ct_ref, dcol_ref, e_ref, w2_ref, al_ref):
    t, cg, p = S5_CHUNK, S5_GROUP, S5_STATE
    hi = lax.Precision.HIGHEST
    lw = t * cg

    def discretise(lam_re, lam_im, log_dt):
        dt = jnp.exp(log_dt)
        mag = jnp.exp(lam_re * dt)
        ang = lam_im * dt
        a_re = mag * jnp.cos(ang)
        a_im = mag * jnp.sin(ang)
        den = lam_re * lam_re + lam_im * lam_im
        f_re = ((a_re - 1) * lam_re + a_im * lam_im) / den
        f_im = (a_im * lam_re - (a_re - 1) * lam_im) / den
        return a_re, a_im, f_re, f_im

    def powers(a_re, a_im):
        out = [(jnp.ones_like(a_re), jnp.zeros_like(a_im))]
        for _ in range(t):
            pr, pi = out[-1]
            out.append((pr * a_re - pi * a_im, pr * a_im + pi * a_re))
        return out

    a_re, a_im, f_re, f_im = discretise(prow_ref[0:1, :], prow_ref[1:2, :], prow_ref[2:3, :])
    pw = powers(a_re, a_im)
    bt_re = bt_ref[0]
    bt_im = bt_ref[1]
    bb_re = f_re * bt_re - f_im * bt_im
    bb_im = f_re * bt_im + f_im * bt_re
    fwd_lane = lax.broadcasted_iota(jnp.int32, (1, LANES), 1) < p
    for j in range(t):
        s_re = jnp.where(fwd_lane, pw[t - 1 - j][0], pw[j][0])
        s_im = jnp.where(fwd_lane, pw[t - 1 - j][1], pw[j][1])
        e_ref[j * cg:(j + 1) * cg, 0:LANES] = (s_re * bb_re - s_im * bb_im).astype(e_ref.dtype)
        e_ref[j * cg:(j + 1) * cg, LANES:2 * LANES] = (s_re * bb_im + s_im * bb_re).astype(e_ref.dtype)
    al_ref[0:1, :] = pw[t][0]
    al_ref[1:2, :] = pw[t][1]

    c_re, c_im, _, _ = discretise(pcol_ref[:, 0:1], pcol_ref[:, 1:2], pcol_ref[:, 2:3])
    pwc = powers(c_re, c_im)
    nb = 2 * t
    wide = nb * cg
    per_col = LANES // cg
    blk = lax.broadcasted_iota(jnp.int32, (2 * p, LANES), 1) // cg
    fwd_row = lax.broadcasted_iota(jnp.int32, (2 * p, LANES), 0) < p
    zero_col = jnp.zeros((2 * p, 1), F32)
    cols_re, cols_im = [], []
    for c0 in range(0, nb, per_col):
        col_re = jnp.zeros((2 * p, LANES), F32)
        col_im = jnp.zeros((2 * p, LANES), F32)
        for bb in range(per_col):
            tf, tr = c0 + bb - (t - 1), t - (c0 + bb)
            fr, fi = pwc[tf] if 0 <= tf <= t else (zero_col, zero_col)
            rr, ri = pwc[tr] if 0 <= tr <= t else (zero_col, zero_col)
            col_re = jnp.where(blk == bb, jnp.where(fwd_row, fr, rr), col_re)
            col_im = jnp.where(blk == bb, jnp.where(fwd_row, fi, ri), col_im)
        cols_re.append(col_re)
        cols_im.append(col_im)
    pw_re = jnp.concatenate(cols_re, axis=1)
    pw_im = jnp.concatenate(cols_im, axis=1)
    ct_re = jnp.concatenate([ct_ref[0]] * (wide // LANES), axis=1)
    ct_im = jnp.concatenate([ct_ref[1]] * (wide // LANES), axis=1)
    rw_re = ct_re * pw_re - ct_im * pw_im
    rw_im = ct_re * pw_im + ct_im * pw_re
    od = w2_ref.dtype
    w2_ref[lw:lw + p, :] = rw_re[0:p, lw:2 * lw].astype(od)
    w2_ref[lw + p:lw + 2 * p, :] = rw_re[p:2 * p, 0:lw].astype(od)
    w2_ref[lw + 2 * p:lw + 3 * p, :] = (-rw_im[0:p, lw:2 * lw]).astype(od)
    w2_ref[lw + 3 * p:lw + 4 * p, :] = (-rw_im[p:2 * p, 0:lw]).astype(od)

    lane_f = lax.broadcasted_iota(jnp.int32, (cg, LANES), 1) < p

    def kern(sel):
        br = jnp.where(sel, bb_re, 0.0)
        bi = jnp.where(sel, bb_im, 0.0)
        return (jnp.dot(br, rw_re, preferred_element_type=F32, precision=hi)
                - jnp.dot(bi, rw_im, preferred_element_type=F32, precision=hi))

    kwf = kern(lane_f)
    kwr = kern(jnp.logical_not(lane_f))
    dcol = jnp.concatenate([dcol_ref[...]] * (lw // LANES), axis=1)
    li = lax.broadcasted_iota(jnp.int32, (cg, lw), 1)
    ri = lax.broadcasted_iota(jnp.int32, (cg, lw), 0)
    for j in range(t):
        mf = pltpu.roll(kwf, (wide - (t - 1 - j) * cg) % wide, axis=1)[:, 0:lw]
        mr = pltpu.roll(kwr, (wide - (t - j) * cg) % wide, axis=1)[:, 0:lw]
        m = mf + mr + jnp.where(li == j * cg + ri, dcol, 0.0)
        w2_ref[j * cg:(j + 1) * cg, :] = m.astype(od)


def _s5_prep_call(lam_re, lam_im, log_dt, b_re, b_im, c_re, c_im, d):
    depth = lam_re.shape[0]
    g, p, cg, t = S5_GROUPS, S5_STATE, S5_GROUP, S5_CHUNK
    both = lambda a: jnp.concatenate([a[:, 0], a[:, 1]], axis=-1)
    rows = lambda a: jnp.concatenate([a[:, 0], a[:, 1]], axis=-2)
    swap = lambda a: a.transpose(0, 1, 2, 4, 3)
    prow = jnp.stack([both(lam_re), both(lam_im),
                      both(jnp.broadcast_to(log_dt[..., None], lam_re.shape))], axis=2).astype(F32)
    pcol = jnp.concatenate([prow.transpose(0, 1, 3, 2), jnp.zeros((depth, g, 2 * p, 5), F32)], axis=-1)
    bt = jnp.stack([both(swap(b_re)), both(swap(b_im))], axis=2).astype(F32)
    ct = jnp.stack([rows(swap(c_re)), rows(swap(c_im))], axis=2).astype(F32)
    ct = jnp.tile(ct, (1, 1, 1, 1, LANES // cg))
    dcol = jnp.broadcast_to(d.astype(F32).reshape(depth, g, cg, 1), (depth, g, cg, LANES))
    lw = t * cg
    spec = lambda a: pl.BlockSpec((None, None) + a.shape[2:], lambda l, gi: (l, gi) + (0,) * (a.ndim - 2))
    e_mat, w2, alpha = pl.pallas_call(
        _s5prep_kernel,
        grid=(depth, g),
        in_specs=[spec(prow), spec(pcol), spec(bt), spec(ct), spec(dcol)],
        out_specs=[pl.BlockSpec((None, None, lw, lw), lambda l, gi: (l, gi, 0, 0)),
                   pl.BlockSpec((None, None, 2 * lw, lw), lambda l, gi: (l, gi, 0, 0)),
                   pl.BlockSpec((None, None, 2, LANES), lambda l, gi: (l, gi, 0, 0))],
        out_shape=[jax.ShapeDtypeStruct((depth, g, lw, lw), BF16),
                   jax.ShapeDtypeStruct((depth, g, 2 * lw, lw), BF16),
                   jax.ShapeDtypeStruct((depth, g, 2, LANES), F32)],
        compiler_params=pltpu.CompilerParams(
            dimension_semantics=("parallel", "parallel"), vmem_limit_bytes=VMEM_LIMIT),
        name="s5prep",
    )(prow, pcol, bt, ct, dcol)
    return e_mat, w2, alpha[:, :, 0], alpha[:, :, 1]


def _cmul_const(xr, xi, c, s):
    def close(a, b):
        return abs(a - b) < 1e-12
    if close(c, 1) and close(s, 0):
        return xr, xi
    if close(c, -1) and close(s, 0):
        return -xr, -xi
    if close(c, 0) and close(s, 1):
        return -xi, xr
    if close(c, 0) and close(s, -1):
        return xi, -xr
    return c * xr - s * xi, c * xi + s * xr


def _fft_list(xs):
    n = len(xs)
    if n == 1:
        return xs
    ev = _fft_list(xs[0::2])
    od = _fft_list(xs[1::2])
    out = [None] * n
    for k in range(n // 2):
        ang = -2.0 * math.pi * k / n
        tr, ti = _cmul_const(od[k][0], od[k][1], math.cos(ang), math.sin(ang))
        out[k] = (ev[k][0] + tr, ev[k][1] + ti)
        out[k + n // 2] = (ev[k][0] - tr, ev[k][1] - ti)
    return out


def _fnet_kernel(x_ref, f_ref, tc_ref, ts_ref, o_ref):
    wb = W_BRANCH
    x = x_ref[...]
    a_re = _bdot(f_ref[0].astype(BF16), x)
    a_im = _bdot(f_ref[1].astype(BF16), x)
    xs = []
    for n in range(FFT_RADIX):
        tc = tc_ref[:, n:n + 1]
        ts = ts_ref[:, n:n + 1]
        ar = a_re[:, n * wb:(n + 1) * wb]
        ai = a_im[:, n * wb:(n + 1) * wb]
        xs.append((ar * tc + ai * ts, ai * tc - ar * ts))
    zs = _fft_list(xs)
    for k2 in range(FFT_RADIX):
        o_ref[k2, :, 0:wb] = zs[k2][0].astype(o_ref.dtype)
        o_ref[k2, :, wb:2 * wb] = zs[k2][1].astype(o_ref.dtype)


@functools.lru_cache(maxsize=None)
def _fnet_consts(n1):
    n = n1 * FFT_RADIX
    k = np.arange(n1, dtype=np.int64)
    ang1 = 2.0 * np.pi * ((k[:, None] * k[None, :]) % n1) / n1
    f = np.stack([np.cos(ang1), -np.sin(ang1)]).astype(np.float32)
    n2 = np.arange(FFT_RADIX, dtype=np.int64)
    ang2 = 2.0 * np.pi * ((k[:, None] * n2[None, :]) % n) / n
    tc = np.cos(ang2).astype(np.float32)
    ts = np.sin(ang2).astype(np.float32)
    c = np.arange(FNET_GROUP, dtype=np.int64)
    angc = 2.0 * np.pi * ((c[:, None] * c[None, :]) % FNET_GROUP) / FNET_GROUP
    norm = 1.0 / math.sqrt(n * FNET_GROUP)
    ng = W_BRANCH // FNET_GROUP
    wcs = np.zeros((2 * W_BRANCH, W_BRANCH), np.float32)
    for gi in range(ng):
        sl = slice(gi * FNET_GROUP, (gi + 1) * FNET_GROUP)
        wcs[sl, sl] = np.cos(angc) * norm
        wcs[W_BRANCH + gi * FNET_GROUP:W_BRANCH + (gi + 1) * FNET_GROUP, sl] = np.sin(angc) * norm
    return f, tc, ts, wcs


def _fnet_call(x):
    b, n1, _ = x.shape
    wb = W_BRANCH
    l = n1 * FFT_RADIX
    tk = min(n1, 256)
    f, tc, ts, _ = _fnet_consts(n1)
    out = pl.pallas_call(
        _fnet_kernel,
        grid=(b, n1 // tk),
        in_specs=[
            pl.BlockSpec((None, n1, FFT_RADIX * wb), lambda bi, i: (bi, 0, 0)),
            pl.BlockSpec((2, tk, n1), lambda bi, i: (0, i, 0)),
            pl.BlockSpec((tk, FFT_RADIX), lambda bi, i: (i, 0)),
            pl.BlockSpec((tk, FFT_RADIX), lambda bi, i: (i, 0)),
        ],
        out_specs=pl.BlockSpec((None, FFT_RADIX, tk, 2 * wb), lambda bi, i: (bi, 0, i, 0)),
        out_shape=jax.ShapeDtypeStruct((b, FFT_RADIX, n1, 2 * wb), BF16),
        compiler_params=pltpu.CompilerParams(
            dimension_semantics=("parallel", "parallel"), vmem_limit_bytes=VMEM_LIMIT),
        name="fnet",
    )(x, jnp.asarray(f), jnp.asarray(tc), jnp.asarray(ts))
    return out.reshape(b, l, 2 * wb)


def _merge_kernel(*refs, seq_len, tn, has_pos, final, mod_row):
    refs = list(refs)
    x_ref = refs.pop(0)
    rowtab_ref, coltile_ref = (refs.pop(0), refs.pop(0)) if has_pos else (None, None)
    mod_ref, g1_ref, g2_ref = refs[0:3]
    refs = refs[3:]
    gf_ref = refs.pop(0) if final else None
    (ys_ref, z_ref, up_prev, up_cur, up_next, v_prev, v_cur, v_next,
     wglu_ref, wcs_ref, wfn_ref, wpool_ref, pscale_ref, cw_ref, cb_ref, lng_ref, lnb_ref, wcv_ref,
     wgate_ref, wbr_ref, wout_ref, w1_ref, w2_ref, o_ref, pbuf, vbuf, sbuf) = refs

    d = D_MODEL
    i = pl.program_id(1)
    nt = pl.num_programs(1)
    x = x_ref[...]
    if has_pos:
        x = x + _pos_tile(rowtab_ref, coltile_ref, i, tn)
    row = pl.program_id(0) if mod_row is None else mod_row
    sh1, sc1, ga1, sh2, sc2, ga2 = _mod_rows(mod_ref, row, N_MOD)
    h = (_rms(x, g1_ref[...]) * (1.0 + sc1) + sh1).astype(BF16)

    first = i == 0
    last = i == nt - 1
    for buf, prev, cur, nxt in ((pbuf, up_prev, up_cur, up_next), (vbuf, v_prev, v_cur, v_next)):
        buf[0:HALO, :] = jnp.where(first, 0.0, prev[...])
        buf[HALO:HALO + tn, :] = cur[...]
        buf[HALO + tn:2 * HALO + tn, :] = jnp.where(last, 0.0, nxt[...])

    def gate_logits(k):
        return _sigmoid(_bdot(h, wgate_ref[:, k * d:(k + 1) * d])).astype(BF16)

    logits = []
    conv_cols = []
    span = tn + 2 * HALO - SUBLANES
    for cl in range(W_BRANCH // LANES):
        logits.append(gate_logits(cl))
        lanes = slice(cl * LANES, (cl + 1) * LANES)
        for s in range(SUBLANES):
            sbuf[s] = vbuf[pl.ds(s, span), lanes]
        acc = None
        for k in range(CONV_WIDTH):
            q, s = divmod(HALO - CONV_WIDTH // 2 + k, SUBLANES)
            term = cw_ref[pl.ds(k, 1), lanes] * sbuf[s, pl.ds(q * SUBLANES, tn), :]
            acc = term if acc is None else acc + term
        conv_cols.append(acc)
    logits.append(gate_logits(2))
    acc = jnp.concatenate(conv_cols, axis=1) + cb_ref[...]
    mu = jnp.mean(acc, axis=-1, keepdims=True)
    xc = acc - mu
    yn = xc * lax.rsqrt(jnp.mean(xc * xc, axis=-1, keepdims=True) + EPS) * lng_ref[...] + lnb_ref[...]
    yn = yn * _sigmoid(yn)
    b_conv = _bdot(yn.astype(BF16), wcv_ref[...])

    t_pos = i * tn + lax.broadcasted_iota(jnp.int32, (tn, LANES), 0)
    lane = lax.broadcasted_iota(jnp.int32, (tn, LANES), 1)
    low = lane < LANES // 2

    def count(half):
        return (jnp.minimum(t_pos + half, seq_len) - jnp.maximum(t_pos - half, 0)).astype(F32)

    def window(col, offsets):
        acc = None
        for off in offsets:
            part = pbuf[pl.ds(HALO + off, tn), col * LANES:(col + 1) * LANES]
            acc = part if acc is None else acc + part
        return acc

    pooled = []
    for col in range(2):
        h_small, h_big = POOL_HALF[2 * col], POOL_HALF[2 * col + 1]
        s_small = window(col, range(-h_small, h_small))
        s_big = s_small + window(col, list(range(-h_big, -h_small)) + list(range(h_small, h_big)))
        mean = jnp.where(low, s_small / count(h_small), s_big / count(h_big))
        pooled.append(mean - up_cur[:, col * LANES:(col + 1) * LANES])
    pooled = jnp.concatenate(pooled, axis=1).astype(BF16)
    b_pool = _bdot(pooled, wpool_ref[...]) * pscale_ref[...]
    logits.append(gate_logits(3))

    y = jax.nn.gelu(jnp.concatenate([ys_ref[0], ys_ref[1]], axis=1))
    b_s5 = y * _sigmoid(_bdot(y.astype(BF16), wglu_ref[...]))

    yf = _bdot(z_ref[...], wcs_ref[...].astype(BF16))
    b_fnet = _bdot(yf.astype(BF16), wfn_ref[...])

    merged = None
    for k, branch in enumerate((b_s5, b_fnet, b_pool, b_conv)):
        part = logits[k] * _bdot(branch.astype(BF16), wbr_ref[k])
        merged = part if merged is None else merged + part

    x1 = x + ga1 * _bdot(merged.astype(BF16), wout_ref[...])

    h2 = (_rms(x1, g2_ref[...]) * (1.0 + sc2) + sh2).astype(BF16)
    acc2 = None
    for cidx in range(D_FF // d):
        a = jnp.maximum(_bdot(h2, w1_ref[:, cidx * d:(cidx + 1) * d]), 0.0)
        part = _bdot((a * a).astype(BF16), w2_ref[cidx * d:(cidx + 1) * d, :])
        acc2 = part if acc2 is None else acc2 + part
    x2 = x1 + ga2 * acc2
    if final:
        x2 = _rms(x2, gf_ref[...])
    o_ref[...] = x2


def _merge_call(x, pos_tabs, mod, g1, g2, gf, ys, z, up, v, wcs, small, big, *, layer, mod_row, tn):
    b, s, d = x.shape
    wb = W_BRANCH
    has_pos = pos_tabs is not None
    final = gf is not None
    nh = tn // HALO
    tok = lambda bi, i: (bi, i, 0)
    prev = lambda bi, i: (bi, jnp.maximum(i * nh - 1, 0), 0)
    nxt = lambda bi, i: (bi, jnp.minimum((i + 1) * nh, s // HALO - 1), 0)

    in_specs = [pl.BlockSpec((None, tn, d), tok)]
    args = [x]
    for a in (tuple(pos_tabs) if has_pos else ()):
        in_specs.append(_param_spec(a, None, single=True))
        args.append(a)
    for a in (mod, g1, g2):
        in_specs.append(_param_spec(a, layer, single=True))
        args.append(a)
    if final:
        in_specs.append(_param_spec(gf, None, single=True))
        args.append(gf)
    in_specs += [pl.BlockSpec((None, 2, tn, LANES), lambda bi, i: (bi, 0, i, 0)),
                 pl.BlockSpec((None, tn, 2 * wb), tok)]
    args += [ys, z]
    for a in (up, v):
        in_specs += [pl.BlockSpec((None, HALO, wb), prev), pl.BlockSpec((None, tn, wb), tok),
                     pl.BlockSpec((None, HALO, wb), nxt)]
        args += [a, a, a]
    for a in small[:1] + (wcs,) + small[1:] + tuple(big):
        in_specs.append(_param_spec(a, None if a is wcs else layer, single=True))
        args.append(a)

    return pl.pallas_call(
        functools.partial(_merge_kernel, seq_len=s, tn=tn, has_pos=has_pos, final=final,
                          mod_row=mod_row),
        grid=(b, s // tn),
        in_specs=in_specs,
        out_specs=pl.BlockSpec((None, tn, d), tok),
        out_shape=jax.ShapeDtypeStruct((b, s, d), F32),
        scratch_shapes=[pltpu.VMEM((tn + 2 * HALO, wb), F32), pltpu.VMEM((tn + 2 * HALO, wb), F32),
                        pltpu.VMEM((SUBLANES, tn + 2 * HALO - SUBLANES, LANES), F32)],
        compiler_params=pltpu.CompilerParams(
            dimension_semantics=("parallel", "parallel"), vmem_limit_bytes=VMEM_LIMIT),
        name="merge",
    )(*args)


def _pos_tables(rows, tn):
    quarter = D_MODEL // 4
    freq = 1.0 / (POS_BASE ** (jnp.arange(quarter, dtype=F32) / quarter))

    def enc(p):
        ang = p.astype(F32)[:, None] * freq[None, :]
        return jnp.concatenate([jnp.sin(ang), jnp.cos(ang)], axis=-1)

    return enc(jnp.arange(rows)), enc(jnp.tile(jnp.arange(GRID_W), tn // GRID_W))


def kernel(x, c, ctx, c_ctx, w_mod, b_mod, g_norm1, w_in, s5_lam_re, s5_lam_im, s5_log_dt, s5_b_re, s5_b_im, s5_c_re, s5_c_im, s5_d, s5_w_glu, fnet_w, pool_w, pool_scale, conv_w, conv_b, conv_ln_g, conv_ln_b, conv_w_out, w_branch, w_out, g_norm2, mlp_w1, mlp_w2, g_final):
    bsz, seq, d = x.shape
    ctx_len = ctx.shape[1]
    depth = w_mod.shape[0]
    wb = W_BRANCH
    assert bsz < 8 and seq % 512 == 0 and ctx_len % (S5_CHUNK * 8) == 0

    tn = 512
    pos_tabs = _pos_tables(seq // GRID_W, tn)
    cpad = jnp.zeros((8, d), F32).at[:bsz].set(c).at[bsz].set(c_ctx)
    mod_all = _mod_call(cpad, w_mod, b_mod)
    ctx_row = bsz
    e_all, w2_all, are_all, aim_all = _s5_prep_call(s5_lam_re, s5_lam_im, s5_log_dt, s5_b_re, s5_b_im,
                                                    s5_c_re, s5_c_im, s5_d)
    vec = lambda a: a.reshape(depth, 1, -1).astype(F32)
    g1_all, g2_all = vec(g_norm1), vec(g_norm2)
    w5_all = w_in[:, :, :5 * wb].astype(BF16)
    pool_bd = jnp.zeros((depth, wb, wb), F32)
    pw_n = pool_w.shape[2]
    for k in range(pool_w.shape[1]):
        pool_bd = pool_bd.at[:, k * pw_n:(k + 1) * pw_n, k * pw_n:(k + 1) * pw_n].set(pool_w[:, k])
    cw_all = jnp.zeros((depth, 32, wb), F32).at[:, :CONV_WIDTH].set(conv_w)
    small = (s5_w_glu.astype(BF16), fnet_w.astype(BF16), pool_bd.astype(BF16), vec(pool_scale), cw_all,
             vec(conv_b), vec(conv_ln_g), vec(conv_ln_b), conv_w_out.astype(BF16))
    big = (w_in[:, :, 5 * wb:].astype(BF16), w_branch.astype(BF16), w_out.astype(BF16),
           mlp_w1.astype(BF16), mlp_w2.astype(BF16))
    gf = g_final.reshape(1, -1).astype(F32)
    wcs = lambda n: jnp.asarray(_fnet_consts(n // FFT_RADIX)[3])

    xc = ctx
    for l in range(depth):
        last = l == depth - 1
        us_c, uf_c, up_c, v_c = _pre_call(xc, None, mod_all, g1_all, w5_all, layer=l, mod_row=ctx_row,
                                          tn=ctx_len)
        us, uf, up, v = _pre_call(x, pos_tabs if l == 0 else None, mod_all, g1_all, w5_all, layer=l,
                                  mod_row=None, tn=tn)

        ys_c, ys = _s5_call(us_c, us, e_all, w2_all, are_all, aim_all, layer=l)

        z = _fnet_call(uf)
        x = _merge_call(x, pos_tabs if l == 0 else None, mod_all, g1_all, g2_all, gf if last else None,
                        ys, z, up, v, wcs(seq), small, big, layer=l, mod_row=None, tn=tn)
        if not last:
            z_c = _fnet_call(uf_c)
            xc = _merge_call(xc, None, mod_all, g1_all, g2_all, None, ys_c, z_c, up_c, v_c, wcs(ctx_len),
                             small, big, layer=l, mod_row=ctx_row, tn=ctx_len)
    return x
```

```python
import functools
import math

import numpy as np
import jax
import jax.numpy as jnp
from jax import lax
from jax.experimental import pallas as pl
from jax.experimental.pallas import tpu as pltpu

F32 = jnp.float32
BF16 = jnp.bfloat16

D_MODEL = 1024
W_BRANCH = 256
N_BRANCH = 4
N_MOD = 6
D_FF = 4 * D_MODEL
S5_GROUPS = 16
S5_GROUP = 16
S5_STATE = 64
FNET_GROUP = 64
GRID_W = 64
CONV_WIDTH = 31
POOL_HALF = (1, 2, 4, 8)
EPS = 1e-6
POS_BASE = 10000.0

S5_CHUNK = 16
S5_STATE_PITCH = 24
FFT_RADIX = 8
HALO = 16
LANES = 128
SUBLANES = 8
VMEM_LIMIT = 56 * 1024 * 1024


def _bdot(a, b):
    return jnp.dot(a, b, preferred_element_type=F32)


def _sigmoid(x):
    return 0.5 * jnp.tanh(0.5 * x) + 0.5


def _rms(x, g):
    return x * lax.rsqrt(jnp.mean(x * x, axis=-1, keepdims=True) + EPS) * g


def _param_spec(a, layer, single=False):
    mode = dict(pipeline_mode=pl.Buffered(1)) if single else {}
    if layer is None:
        return pl.BlockSpec(a.shape, lambda *_: (0,) * a.ndim, **mode)
    return pl.BlockSpec((None,) + a.shape[1:], lambda *_: (layer,) + (0,) * (a.ndim - 1), **mode)


def _mod_rows(mod_ref, row, n):
    return [mod_ref[pl.ds(row, 1), k * D_MODEL:(k + 1) * D_MODEL] for k in range(n)]


def _mod_kernel(c_ref, w_ref, b_ref, o_ref):
    c = c_ref[...]
    act = c * jax.nn.sigmoid(c)
    o_ref[...] = _bdot(act.astype(BF16), w_ref[...].astype(BF16)) + b_ref[...]


def _mod_call(cpad, w_mod, b_mod):
    depth, d, n = w_mod.shape
    tc = 1536
    return pl.pallas_call(
        _mod_kernel,
        grid=(depth, n // tc),
        in_specs=[
            pl.BlockSpec((8, d), lambda l, j: (0, 0)),
            pl.BlockSpec((None, d, tc), lambda l, j: (l, 0, j)),
            pl.BlockSpec((None, 1, tc), lambda l, j: (l, 0, j)),
        ],
        out_specs=pl.BlockSpec((None, 8, tc), lambda l, j: (l, 0, j)),
        out_shape=jax.ShapeDtypeStruct((depth, 8, n), F32),
        compiler_params=pltpu.CompilerParams(
            dimension_semantics=("parallel", "parallel"), vmem_limit_bytes=VMEM_LIMIT),
        name="mod",
    )(cpad, w_mod, b_mod.reshape(depth, 1, n))


def _cast_kernel(win_ref, w1_ref, w2_ref, wbr_ref, wout_ref, o5_ref, og_ref, o1_ref, o2_ref, obr_ref, oout_ref):
    split = o5_ref.shape[-1]
    o5_ref[...] = win_ref[:, 0:split].astype(BF16)
    og_ref[...] = win_ref[:, split:].astype(BF16)
    for src, dst in ((w1_ref, o1_ref), (w2_ref, o2_ref), (wbr_ref, obr_ref), (wout_ref, oout_ref)):
        dst[...] = src[...].astype(BF16)


def _cast_call(w_in, w1, w2, wbr, wout, split):
    depth = w_in.shape[0]
    steps = 8
    srcs = (w_in, w1, w2, wbr.reshape(depth, -1, wbr.shape[-1]), wout)
    rows = lambda a: a.shape[1] // steps
    spec = lambda a, cols: pl.BlockSpec((None, rows(a), cols), lambda l, i: (l, i, 0))
    outs = [(w_in, split), (w_in, w_in.shape[2] - split)] + [(a, a.shape[2]) for a in srcs[1:]]
    res = pl.pallas_call(
        _cast_kernel,
        grid=(depth, steps),
        in_specs=[spec(a, a.shape[2]) for a in srcs],
        out_specs=[spec(a, cols) for a, cols in outs],
        out_shape=[jax.ShapeDtypeStruct((depth, a.shape[1], cols), BF16) for a, cols in outs],
        compiler_params=pltpu.CompilerParams(
            dimension_semantics=("parallel", "parallel"), vmem_limit_bytes=VMEM_LIMIT),
        name="cast",
    )(*srcs)
    w5, wg, w1b, w2b, wbrb, woutb = res
    return w5, wg, w1b, w2b, wbrb.reshape(wbr.shape), woutb


def _pos_tile(rowtab_ref, coltile_ref, tile_idx, tn):
    rows_per_tile = tn // GRID_W
    half = D_MODEL // 2
    parts = [jnp.broadcast_to(rowtab_ref[pl.ds(tile_idx * rows_per_tile + q, 1), :], (GRID_W, half))
             for q in range(rows_per_tile)]
    return jnp.concatenate([jnp.concatenate(parts, axis=0), coltile_ref[...]], axis=1)


def _pre_kernel(*refs, has_pos, mod_row):
    if has_pos:
        (x_ref, rowtab_ref, coltile_ref, mod_ref, g1_ref, w5_ref,
         us_ref, uf_ref, up_ref, v_ref, fbuf) = refs
        x = x_ref[...] + _pos_tile(rowtab_ref, coltile_ref, pl.program_id(1), x_ref.shape[0])
    else:
        x_ref, mod_ref, g1_ref, w5_ref, us_ref, uf_ref, up_ref, v_ref, fbuf = refs
        x = x_ref[...]
    row = pl.program_id(0) if mod_row is None else mod_row
    sh, sc = _mod_rows(mod_ref, row, 2)
    h = (_rms(x, g1_ref[...]) * (1.0 + sc) + sh).astype(BF16)
    z = _bdot(h, w5_ref[...])
    wb = W_BRANCH
    tn = z.shape[0]
    us_ref[0] = z[:, 0:LANES]
    us_ref[1] = z[:, LANES:wb]
    fbuf[0] = z[:, wb:wb + LANES]
    fbuf[1] = z[:, wb + LANES:2 * wb]
    for n2 in range(FFT_RADIX):
        for hf in range(2):
            lo = n2 * wb + hf * LANES
            uf_ref[:, lo:lo + LANES] = fbuf[hf, pl.ds(n2, tn // FFT_RADIX, stride=FFT_RADIX), :].astype(BF16)
    up_ref[...] = z[:, 2 * wb:3 * wb]
    v_ref[...] = z[:, 3 * wb:4 * wb] * jax.nn.sigmoid(z[:, 4 * wb:5 * wb])


def _pre_call(x, pos_tabs, mod, g1, w5, *, layer, mod_row, tn):
    b, s, d = x.shape
    wb = W_BRANCH
    has_pos = pos_tabs is not None
    tok = lambda bi, i: (bi, i, 0)
    in_specs = [pl.BlockSpec((None, tn, d), tok)]
    args = [x]
    if has_pos:
        in_specs += [_param_spec(a, None) for a in pos_tabs]
        args += list(pos_tabs)
    in_specs += [_param_spec(a, layer) for a in (mod, g1, w5)]
    args += [mod, g1, w5]
    out_spec = pl.BlockSpec((None, tn, wb), tok)
    return pl.pallas_call(
        functools.partial(_pre_kernel, has_pos=has_pos, mod_row=mod_row),
        grid=(b, s // tn),
        in_specs=in_specs,
        out_specs=[
            pl.BlockSpec((None, 2, tn, LANES), lambda bi, i: (bi, 0, i, 0)),
            pl.BlockSpec((None, tn // FFT_RADIX, FFT_RADIX * wb), tok),
            out_spec, out_spec,
        ],
        out_shape=[
            jax.ShapeDtypeStruct((b, 2, s, LANES), F32),
            jax.ShapeDtypeStruct((b, s // FFT_RADIX, FFT_RADIX * wb), BF16),
            jax.ShapeDtypeStruct((b, s, wb), F32),
            jax.ShapeDtypeStruct((b, s, wb), F32),
        ],
        scratch_shapes=[pltpu.VMEM((2, tn, LANES), F32)],
        compiler_params=pltpu.CompilerParams(
            dimension_semantics=("parallel", "parallel"), vmem_limit_bytes=VMEM_LIMIT),
        name="pre",
    )(*args)


def _transpose_lane_blocks(pieces):
    xs = list(pieces)
    blk = lax.broadcasted_iota(jnp.int32, xs[0].shape, 1) // S5_GROUP
    for dist in (4, 2, 1):
        upper = (blk & dist) != 0
        for i in range(8):
            if i & dist:
                continue
            a, b = xs[i], xs[i + dist]
            xs[i] = jnp.where(upper, pltpu.roll(b, dist * S5_GROUP, axis=1), a)
            xs[i + dist] = jnp.where(upper, b, pltpu.roll(a, LANES - dist * S5_GROUP, axis=1))
    return xs


def _s5_kernel(usc_ref, us_ref, e_ref, w2_ref, ar_ref, ai_ref, yc_ref, y_ref, ubuf, sre, sim, ybuf,
               *, n_ctx_chunks):
    groups, nc, _ = ubuf.shape
    t = S5_CHUNK
    half = LANES // 2
    n_lat_chunks = nc - n_ctx_chunks
    rb = 64
    pitch = S5_STATE_PITCH

    def relayout_in(src_ref, src_chunk0, nrows, dst_row0):
        for hf in range(2):
            for jj in range(2):
                pieces = [src_ref[hf, pl.ds(src_chunk0 * t + jj * 8 + m, nrows, stride=t), :]
                          for m in range(8)]
                for g8, col in enumerate(_transpose_lane_blocks(pieces)):
                    ubuf[hf * 8 + g8, pl.ds(dst_row0, nrows), jj * LANES:(jj + 1) * LANES] = col.astype(BF16)

    relayout_in(usc_ref, 0, n_ctx_chunks, 0)

    def in_body(i, carry):
        r0 = pl.multiple_of(i * rb, rb)
        relayout_in(us_ref, r0, rb, pl.multiple_of(n_ctx_chunks + r0, 16))
        return carry

    lax.fori_loop(0, n_lat_chunks // rb, in_body, 0)

    for g in range(groups):
        e = _bdot(ubuf[g], e_ref[g])
        sre[pl.ds(g, nc, stride=pitch), :] = e[:, 0:LANES]
        sim[pl.ds(g, nc, stride=pitch), :] = e[:, LANES:2 * LANES]

    a_re = ar_ref[...]
    a_im = ai_ref[...]
    is_fwd = lax.broadcasted_iota(jnp.int32, (groups, LANES), 1) < half

    def step(i, carry):
        s_re, s_im = carry
        kf = i
        kr = jnp.where(i < n_ctx_chunks, n_ctx_chunks - 1 - i, nc - 1 + n_ctx_chunks - i)
        rf = pl.multiple_of(kf * pitch, SUBLANES)
        rr = pl.multiple_of(kr * pitch, SUBLANES)
        e_re = jnp.where(is_fwd, sre[pl.ds(rf, groups), :], sre[pl.ds(rr, groups), :])
        e_im = jnp.where(is_fwd, sim[pl.ds(rf, groups), :], sim[pl.ds(rr, groups), :])
        sre[pl.ds(rf, groups), 0:half] = s_re[:, 0:half]
        sim[pl.ds(rf, groups), 0:half] = s_im[:, 0:half]
        sre[pl.ds(rr, groups), half:LANES] = s_re[:, half:LANES]
        sim[pl.ds(rr, groups), half:LANES] = s_im[:, half:LANES]
        n_re = a_re * s_re - a_im * s_im + e_re
        n_im = a_re * s_im + a_im * s_re + e_im
        return n_re, n_im

    zero = jnp.zeros((groups, LANES), F32)
    lax.fori_loop(0, nc, step, (zero, zero))

    for hf in range(2):
        for g8 in range(8):
            g = hf * 8 + g8
            s_re = sre[pl.ds(g, nc, stride=pitch), :].astype(BF16)
            s_im = sim[pl.ds(g, nc, stride=pitch), :].astype(BF16)
            lhs = jnp.concatenate([ubuf[g], s_re, s_im], axis=1)
            ybuf[g8] = _bdot(lhs, w2_ref[g])

        def relayout_out(dst_ref, src_row0, nrows, dst_chunk0):
            for ii in range(2):
                pieces = [ybuf[q, pl.ds(src_row0, nrows), ii * LANES:(ii + 1) * LANES] for q in range(8)]
                for m, tok in enumerate(_transpose_lane_blocks(pieces)):
                    dst_ref[hf, pl.ds(dst_chunk0 * t + ii * 8 + m, nrows, stride=t), :] = tok

        relayout_out(yc_ref, 0, n_ctx_chunks, 0)

        def out_body(i, carry):
            r0 = pl.multiple_of(i * rb, rb)
            relayout_out(y_ref, pl.multiple_of(n_ctx_chunks + r0, 8), rb, r0)
            return carry

        lax.fori_loop(0, n_lat_chunks // rb, out_body, 0)


def _s5_call(us_c, us, e_mat, w2, a_re, a_im, *, layer):
    b, _, lc, _ = us_c.shape
    s = us.shape[2]
    t = S5_CHUNK
    n_ctx_chunks = lc // t
    nc = n_ctx_chunks + s // t
    groups = S5_GROUPS
    lw = t * S5_GROUP
    single = pl.Buffered(1)
    slab = lambda n: pl.BlockSpec((None, 2, n, LANES), lambda bi: (bi, 0, 0, 0))
    slab1 = lambda n: pl.BlockSpec((None, 2, n, LANES), lambda bi: (bi, 0, 0, 0), pipeline_mode=single)
    return pl.pallas_call(
        functools.partial(_s5_kernel, n_ctx_chunks=n_ctx_chunks),
        grid=(b,),
        in_specs=[slab1(lc), slab1(s), _param_spec(e_mat, layer, single=True),
                  _param_spec(w2, layer, single=True), _param_spec(a_re, layer), _param_spec(a_im, layer)],
        out_specs=[slab(lc), slab(s)],
        out_shape=[jax.ShapeDtypeStruct((b, 2, lc, LANES), F32),
                   jax.ShapeDtypeStruct((b, 2, s, LANES), F32)],
        scratch_shapes=[pltpu.VMEM((groups, nc, lw), BF16),
                        pltpu.VMEM((nc * S5_STATE_PITCH, LANES), F32),
                        pltpu.VMEM((nc * S5_STATE_PITCH, LANES), F32),
                        pltpu.VMEM((8, nc, lw), F32)],
        compiler_params=pltpu.CompilerParams(
            dimension_semantics=("parallel",), vmem_limit_bytes=VMEM_LIMIT),
        name="s5",
    )(us_c, us, e_mat, w2, a_re, a_im)


def _s5prep_kernel(prow_ref, bt_ref, ct_ref, dcol_ref, e_ref, w2_ref, al_ref, pbuf):
    t, cg, p = S5_CHUNK, S5_GROUP, S5_STATE
    lw = t * cg

    def discretise(lam_re, lam_im, log_dt):
        dt = jnp.exp(log_dt)
        mag = jnp.exp(lam_re * dt)
        ang = lam_im * dt
        a_re = mag * jnp.cos(ang)
        a_im = mag * jnp.sin(ang)
        den = lam_re * lam_re + lam_im * lam_im
        f_re = ((a_re - 1) * lam_re + a_im * lam_im) / den
        f_im = (a_im * lam_re - (a_re - 1) * lam_im) / den
        return a_re, a_im, f_re, f_im

    def powers(a_re, a_im):
        out = [(jnp.ones_like(a_re), jnp.zeros_like(a_im))]
        for _ in range(t):
            pr, pi = out[-1]
            out.append((pr * a_re - pi * a_im, pr * a_im + pi * a_re))
        return out

    a_re, a_im, f_re, f_im = discretise(prow_ref[0:1, :], prow_ref[1:2, :], prow_ref[2:3, :])
    pw = powers(a_re, a_im)
    bt_re = bt_ref[0]
    bt_im = bt_ref[1]
    bb_re = f_re * bt_re - f_im * bt_im
    bb_im = f_re * bt_im + f_im * bt_re
    fwd_lane = lax.broadcasted_iota(jnp.int32, (1, LANES), 1) < p
    for j in range(t):
        s_re = jnp.where(fwd_lane, pw[t - 1 - j][0], pw[j][0])
        s_im = jnp.where(fwd_lane, pw[t - 1 - j][1], pw[j][1])
        e_ref[j * cg:(j + 1) * cg, 0:LANES] = (s_re * bb_re - s_im * bb_im).astype(e_ref.dtype)
        e_ref[j * cg:(j + 1) * cg, LANES:2 * LANES] = (s_re * bb_im + s_im * bb_re).astype(e_ref.dtype)
    al_ref[0:1, :] = pw[t][0]
    al_ref[1:2, :] = pw[t][1]

    pbuf[...] = jnp.zeros_like(pbuf)
    for i in range(t + 1):
        pbuf[i:i + 1, :] = pw[i][0]
        pbuf[t + 1 + i:t + 2 + i, :] = pw[i][1]
    pw_t = jnp.transpose(pbuf[...])
    pwc = [(pw_t[:, i:i + 1], pw_t[:, t + 1 + i:t + 2 + i]) for i in range(t + 1)]
    nb = 2 * t
    wide = nb * cg
    per_col = LANES // cg
    blk = lax.broadcasted_iota(jnp.int32, (2 * p, LANES), 1) // cg
    fwd_row = lax.broadcasted_iota(jnp.int32, (2 * p, LANES), 0) < p
    zero_col = jnp.zeros((2 * p, 1), F32)
    cols_re, cols_im = [], []
    for c0 in range(0, nb, per_col):
        col_re = jnp.zeros((2 * p, LANES), F32)
        col_im = jnp.zeros((2 * p, LANES), F32)
        for bb in range(per_col):
            tf, tr = c0 + bb - (t - 1), t - (c0 + bb)
            fr, fi = pwc[tf] if 0 <= tf <= t else (zero_col, zero_col)
            rr, ri = pwc[tr] if 0 <= tr <= t else (zero_col, zero_col)
            col_re = jnp.where(blk == bb, jnp.where(fwd_row, fr, rr), col_re)
            col_im = jnp.where(blk == bb, jnp.where(fwd_row, fi, ri), col_im)
        cols_re.append(col_re)
        cols_im.append(col_im)
    pw_re = jnp.concatenate(cols_re, axis=1)
    pw_im = jnp.concatenate(cols_im, axis=1)
    ct_re = jnp.concatenate([ct_ref[0]] * (wide // LANES), axis=1)
    ct_im = jnp.concatenate([ct_ref[1]] * (wide // LANES), axis=1)
    rw_re = ct_re * pw_re - ct_im * pw_im
    rw_im = ct_re * pw_im + ct_im * pw_re
    od = w2_ref.dtype
    w2_ref[lw:lw + p, :] = rw_re[0:p, lw:2 * lw].astype(od)
    w2_ref[lw + p:lw + 2 * p, :] = rw_re[p:2 * p, 0:lw].astype(od)
    w2_ref[lw + 2 * p:lw + 3 * p, :] = (-rw_im[0:p, lw:2 * lw]).astype(od)
    w2_ref[lw + 3 * p:lw + 4 * p, :] = (-rw_im[p:2 * p, 0:lw]).astype(od)

    lane_f = lax.broadcasted_iota(jnp.int32, (cg, LANES), 1) < p

    def rows(sel):
        return jnp.concatenate([jnp.where(sel, bb_re, 0.0), -jnp.where(sel, bb_im, 0.0)], axis=1)

    lhs = jnp.concatenate([rows(lane_f), rows(jnp.logical_not(lane_f))], axis=0)
    rhs = jnp.concatenate([rw_re, rw_im], axis=0)
    lhs_hi, rhs_hi = lhs.astype(BF16), rhs.astype(BF16)
    lhs_lo = (lhs - lhs_hi.astype(F32)).astype(BF16)
    rhs_lo = (rhs - rhs_hi.astype(F32)).astype(BF16)
    kw = _bdot(lhs_hi, rhs_hi) + _bdot(lhs_lo, rhs_hi) + _bdot(lhs_hi, rhs_lo)
    kwf = kw[0:cg]
    kwr = kw[cg:2 * cg]
    dcol = jnp.concatenate([dcol_ref[...]] * (lw // LANES), axis=1)
    li = lax.broadcasted_iota(jnp.int32, (cg, lw), 1)
    ri = lax.broadcasted_iota(jnp.int32, (cg, lw), 0)
    for j in range(t):
        mf = pltpu.roll(kwf, (wide - (t - 1 - j) * cg) % wide, axis=1)[:, 0:lw]
        mr = pltpu.roll(kwr, (wide - (t - j) * cg) % wide, axis=1)[:, 0:lw]
        m = mf + mr + jnp.where(li == j * cg + ri, dcol, 0.0)
        w2_ref[j * cg:(j + 1) * cg, :] = m.astype(od)


def _s5_prep_call(lam_re, lam_im, log_dt, b_re, b_im, c_re, c_im, d):
    depth = lam_re.shape[0]
    g, p, cg, t = S5_GROUPS, S5_STATE, S5_GROUP, S5_CHUNK
    both = lambda a: jnp.concatenate([a[:, 0], a[:, 1]], axis=-1)
    rows = lambda a: jnp.concatenate([a[:, 0], a[:, 1]], axis=-2)
    swap = lambda a: a.transpose(0, 1, 2, 4, 3)
    prow = jnp.stack([both(lam_re), both(lam_im),
                      both(jnp.broadcast_to(log_dt[..., None], lam_re.shape))], axis=2).astype(F32)
    bt = jnp.stack([both(swap(b_re)), both(swap(b_im))], axis=2).astype(F32)
    ct = jnp.stack([rows(swap(c_re)), rows(swap(c_im))], axis=2).astype(F32)
    ct = jnp.tile(ct, (1, 1, 1, 1, LANES // cg))
    dcol = jnp.broadcast_to(d.astype(F32).reshape(depth, g, cg, 1), (depth, g, cg, LANES))
    lw = t * cg
    spec = lambda a: pl.BlockSpec((None, None) + a.shape[2:], lambda l, gi: (l, gi) + (0,) * (a.ndim - 2))
    e_mat, w2, alpha = pl.pallas_call(
        _s5prep_kernel,
        grid=(depth, g),
        in_specs=[spec(prow), spec(bt), spec(ct), spec(dcol)],
        out_specs=[pl.BlockSpec((None, None, lw, lw), lambda l, gi: (l, gi, 0, 0)),
                   pl.BlockSpec((None, None, 2 * lw, lw), lambda l, gi: (l, gi, 0, 0)),
                   pl.BlockSpec((None, None, 2, LANES), lambda l, gi: (l, gi, 0, 0))],
        out_shape=[jax.ShapeDtypeStruct((depth, g, lw, lw), BF16),
                   jax.ShapeDtypeStruct((depth, g, 2 * lw, lw), BF16),
                   jax.ShapeDtypeStruct((depth, g, 2, LANES), F32)],
        scratch_shapes=[pltpu.VMEM((2 * p, 2 * p), F32)],
        compiler_params=pltpu.CompilerParams(
            dimension_semantics=("parallel", "parallel"), vmem_limit_bytes=VMEM_LIMIT),
        name="s5prep",
    )(prow, bt, ct, dcol)
    return e_mat, w2, alpha[:, :, 0], alpha[:, :, 1]


def _cmul_const(xr, xi, c, s):
    def close(a, b):
        return abs(a - b) < 1e-12
    if close(c, 1) and close(s, 0):
        return xr, xi
    if close(c, -1) and close(s, 0):
        return -xr, -xi
    if close(c, 0) and close(s, 1):
        return -xi, xr
    if close(c, 0) and close(s, -1):
        return xi, -xr
    return c * xr - s * xi, c * xi + s * xr


def _fft_list(xs):
    n = len(xs)
    if n == 1:
        return xs
    ev = _fft_list(xs[0::2])
    od = _fft_list(xs[1::2])
    out = [None] * n
    for k in range(n // 2):
        ang = -2.0 * math.pi * k / n
        tr, ti = _cmul_const(od[k][0], od[k][1], math.cos(ang), math.sin(ang))
        out[k] = (ev[k][0] + tr, ev[k][1] + ti)
        out[k + n // 2] = (ev[k][0] - tr, ev[k][1] - ti)
    return out


def _fnet_kernel(x_ref, f_ref, tc_ref, ts_ref, o_ref):
    wb = W_BRANCH
    x = x_ref[...]
    a_re = _bdot(f_ref[0].astype(BF16), x)
    a_im = _bdot(f_ref[1].astype(BF16), x)
    xs = []
    for n in range(FFT_RADIX):
        tc = tc_ref[:, n:n + 1]
        ts = ts_ref[:, n:n + 1]
        ar = a_re[:, n * wb:(n + 1) * wb]
        ai = a_im[:, n * wb:(n + 1) * wb]
        xs.append((ar * tc + ai * ts, ai * tc - ar * ts))
    zs = _fft_list(xs)
    for k2 in range(FFT_RADIX):
        o_ref[k2, :, 0:wb] = zs[k2][0].astype(o_ref.dtype)
        o_ref[k2, :, wb:2 * wb] = zs[k2][1].astype(o_ref.dtype)


@functools.lru_cache(maxsize=None)
def _fnet_consts(n1):
    n = n1 * FFT_RADIX
    k = np.arange(n1, dtype=np.int64)
    ang1 = 2.0 * np.pi * ((k[:, None] * k[None, :]) % n1) / n1
    f = np.stack([np.cos(ang1), -np.sin(ang1)]).astype(np.float32)
    n2 = np.arange(FFT_RADIX, dtype=np.int64)
    ang2 = 2.0 * np.pi * ((k[:, None] * n2[None, :]) % n) / n
    tc = np.cos(ang2).astype(np.float32)
    ts = np.sin(ang2).astype(np.float32)
    c = np.arange(FNET_GROUP, dtype=np.int64)
    angc = 2.0 * np.pi * ((c[:, None] * c[None, :]) % FNET_GROUP) / FNET_GROUP
    norm = 1.0 / math.sqrt(n * FNET_GROUP)
    ng = W_BRANCH // FNET_GROUP
    wcs = np.zeros((2 * W_BRANCH, W_BRANCH), np.float32)
    for gi in range(ng):
        sl = slice(gi * FNET_GROUP, (gi + 1) * FNET_GROUP)
        wcs[sl, sl] = np.cos(angc) * norm
        wcs[W_BRANCH + gi * FNET_GROUP:W_BRANCH + (gi + 1) * FNET_GROUP, sl] = np.sin(angc) * norm
    return f, tc, ts, wcs


def _fnet_call(x):
    b, n1, _ = x.shape
    wb = W_BRANCH
    l = n1 * FFT_RADIX
    tk = min(n1, 256)
    f, tc, ts, _ = _fnet_consts(n1)
    out = pl.pallas_call(
        _fnet_kernel,
        grid=(b, n1 // tk),
        in_specs=[
            pl.BlockSpec((None, n1, FFT_RADIX * wb), lambda bi, i: (bi, 0, 0)),
            pl.BlockSpec((2, tk, n1), lambda bi, i: (0, i, 0)),
            pl.BlockSpec((tk, FFT_RADIX), lambda bi, i: (i, 0)),
            pl.BlockSpec((tk, FFT_RADIX), lambda bi, i: (i, 0)),
        ],
        out_specs=pl.BlockSpec((None, FFT_RADIX, tk, 2 * wb), lambda bi, i: (bi, 0, i, 0)),
        out_shape=jax.ShapeDtypeStruct((b, FFT_RADIX, n1, 2 * wb), BF16),
        compiler_params=pltpu.CompilerParams(
            dimension_semantics=("parallel", "parallel"), vmem_limit_bytes=VMEM_LIMIT),
        name="fnet",
    )(x, jnp.asarray(f), jnp.asarray(tc), jnp.asarray(ts))
    return out.reshape(b, l, 2 * wb)


def _merge_kernel(*refs, seq_len, tn, has_pos, final, mod_row):
    refs = list(refs)
    x_ref = refs.pop(0)
    rowtab_ref, coltile_ref = (refs.pop(0), refs.pop(0)) if has_pos else (None, None)
    mod_ref, g1_ref, g2_ref = refs[0:3]
    refs = refs[3:]
    gf_ref = refs.pop(0) if final else None
    (ys_ref, z_ref, up_prev, up_cur, up_next, v_prev, v_cur, v_next,
     wglu_ref, wcs_ref, wfn_ref, wpool_ref, pscale_ref, cw_ref, cb_ref, lng_ref, lnb_ref, wcv_ref,
     wgate_ref, wbr_ref, wout_ref, w1_ref, w2_ref, o_ref, pbuf, vbuf, sbuf) = refs

    d = D_MODEL
    i = pl.program_id(1)
    nt = pl.num_programs(1)
    x = x_ref[...]
    if has_pos:
        x = x + _pos_tile(rowtab_ref, coltile_ref, i, tn)
    row = pl.program_id(0) if mod_row is None else mod_row
    sh1, sc1, ga1, sh2, sc2, ga2 = _mod_rows(mod_ref, row, N_MOD)
    h = (_rms(x, g1_ref[...]) * (1.0 + sc1) + sh1).astype(BF16)

    first = i == 0
    last = i == nt - 1
    for buf, prev, cur, nxt in ((pbuf, up_prev, up_cur, up_next), (vbuf, v_prev, v_cur, v_next)):
        buf[0:HALO, :] = jnp.where(first, 0.0, prev[...])
        buf[HALO:HALO + tn, :] = cur[...]
        buf[HALO + tn:2 * HALO + tn, :] = jnp.where(last, 0.0, nxt[...])

    def gate_logits(k):
        return _sigmoid(_bdot(h, wgate_ref[:, k * d:(k + 1) * d])).astype(BF16)

    logits = []
    conv_cols = []
    span = tn + 2 * HALO - SUBLANES
    for cl in range(W_BRANCH // LANES):
        logits.append(gate_logits(cl))
        lanes = slice(cl * LANES, (cl + 1) * LANES)
        for s in range(SUBLANES):
            sbuf[s] = vbuf[pl.ds(s, span), lanes]
        acc = None
        for k in range(CONV_WIDTH):
            q, s = divmod(HALO - CONV_WIDTH // 2 + k, SUBLANES)
            term = cw_ref[pl.ds(k, 1), lanes] * sbuf[s, pl.ds(q * SUBLANES, tn), :]
            acc = term if acc is None else acc + term
        conv_cols.append(acc)
    logits.append(gate_logits(2))
    acc = jnp.concatenate(conv_cols, axis=1) + cb_ref[...]
    mu = jnp.mean(acc, axis=-1, keepdims=True)
    xc = acc - mu
    yn = xc * lax.rsqrt(jnp.mean(xc * xc, axis=-1, keepdims=True) + EPS) * lng_ref[...] + lnb_ref[...]
    yn = yn * _sigmoid(yn)
    b_conv = _bdot(yn.astype(BF16), wcv_ref[...])

    t_pos = i * tn + lax.broadcasted_iota(jnp.int32, (tn, LANES), 0)
    lane = lax.broadcasted_iota(jnp.int32, (tn, LANES), 1)
    low = lane < LANES // 2

    def count(half):
        return (jnp.minimum(t_pos + half, seq_len) - jnp.maximum(t_pos - half, 0)).astype(F32)

    def window(col, offsets):
        acc = None
        for off in offsets:
            part = pbuf[pl.ds(HALO + off, tn), col * LANES:(col + 1) * LANES]
            acc = part if acc is None else acc + part
        return acc

    pooled = []
    for col in range(2):
        h_small, h_big = POOL_HALF[2 * col], POOL_HALF[2 * col + 1]
        s_small = window(col, range(-h_small, h_small))
        s_big = s_small + window(col, list(range(-h_big, -h_small)) + list(range(h_small, h_big)))
        mean = jnp.where(low, s_small / count(h_small), s_big / count(h_big))
        pooled.append(mean - up_cur[:, col * LANES:(col + 1) * LANES])
    pooled = jnp.concatenate(pooled, axis=1).astype(BF16)
    b_pool = _bdot(pooled, wpool_ref[...]) * pscale_ref[...]
    logits.append(gate_logits(3))

    y = jax.nn.gelu(jnp.concatenate([ys_ref[0], ys_ref[1]], axis=1))
    b_s5 = y * _sigmoid(_bdot(y.astype(BF16), wglu_ref[...]))

    yf = _bdot(z_ref[...], wcs_ref[...].astype(BF16))
    b_fnet = _bdot(yf.astype(BF16), wfn_ref[...])

    merged = None
    for k, branch in enumerate((b_s5, b_fnet, b_pool, b_conv)):
        part = logits[k] * _bdot(branch.astype(BF16), wbr_ref[k])
        merged = part if merged is None else merged + part

    x1 = x + ga1 * _bdot(merged.astype(BF16), wout_ref[...])

    h2 = (_rms(x1, g2_ref[...]) * (1.0 + sc2) + sh2).astype(BF16)
    acc2 = None
    for cidx in range(D_FF // d):
        a = jnp.maximum(_bdot(h2, w1_ref[:, cidx * d:(cidx + 1) * d]), 0.0)
        part = _bdot((a * a).astype(BF16), w2_ref[cidx * d:(cidx + 1) * d, :])
        acc2 = part if acc2 is None else acc2 + part
    x2 = x1 + ga2 * acc2
    if final:
        x2 = _rms(x2, gf_ref[...])
    o_ref[...] = x2


def _merge_call(x, pos_tabs, mod, g1, g2, gf, ys, z, up, v, wcs, small, big, *, layer, mod_row, tn):
    b, s, d = x.shape
    wb = W_BRANCH
    has_pos = pos_tabs is not None
    final = gf is not None
    nh = tn // HALO
    tok = lambda bi, i: (bi, i, 0)
    prev = lambda bi, i: (bi, jnp.maximum(i * nh - 1, 0), 0)
    nxt = lambda bi, i: (bi, jnp.minimum((i + 1) * nh, s // HALO - 1), 0)

    in_specs = [pl.BlockSpec((None, tn, d), tok)]
    args = [x]
    for a in (tuple(pos_tabs) if has_pos else ()):
        in_specs.append(_param_spec(a, None, single=True))
        args.append(a)
    for a in (mod, g1, g2):
        in_specs.append(_param_spec(a, layer, single=True))
        args.append(a)
    if final:
        in_specs.append(_param_spec(gf, None, single=True))
        args.append(gf)
    in_specs += [pl.BlockSpec((None, 2, tn, LANES), lambda bi, i: (bi, 0, i, 0)),
                 pl.BlockSpec((None, tn, 2 * wb), tok)]
    args += [ys, z]
    for a in (up, v):
        in_specs += [pl.BlockSpec((None, HALO, wb), prev), pl.BlockSpec((None, tn, wb), tok),
                     pl.BlockSpec((None, HALO, wb), nxt)]
        args += [a, a, a]
    for a in small[:1] + (wcs,) + small[1:] + tuple(big):
        in_specs.append(_param_spec(a, None if a is wcs else layer, single=True))
        args.append(a)

    return pl.pallas_call(
        functools.partial(_merge_kernel, seq_len=s, tn=tn, has_pos=has_pos, final=final,
                          mod_row=mod_row),
        grid=(b, s // tn),
        in_specs=in_specs,
        out_specs=pl.BlockSpec((None, tn, d), tok),
        out_shape=jax.ShapeDtypeStruct((b, s, d), F32),
        scratch_shapes=[pltpu.VMEM((tn + 2 * HALO, wb), F32), pltpu.VMEM((tn + 2 * HALO, wb), F32),
                        pltpu.VMEM((SUBLANES, tn + 2 * HALO - SUBLANES, LANES), F32)],
        compiler_params=pltpu.CompilerParams(
            dimension_semantics=("parallel", "parallel"), vmem_limit_bytes=VMEM_LIMIT),
        name="merge",
    )(*args)


def _pos_tables(rows, tn):
    quarter = D_MODEL // 4
    freq = 1.0 / (POS_BASE ** (jnp.arange(quarter, dtype=F32) / quarter))

    def enc(p):
        ang = p.astype(F32)[:, None] * freq[None, :]
        return jnp.concatenate([jnp.sin(ang), jnp.cos(ang)], axis=-1)

    return enc(jnp.arange(rows)), enc(jnp.tile(jnp.arange(GRID_W), tn // GRID_W))


def kernel(x, c, ctx, c_ctx, w_mod, b_mod, g_norm1, w_in, s5_lam_re, s5_lam_im, s5_log_dt, s5_b_re, s5_b_im, s5_c_re, s5_c_im, s5_d, s5_w_glu, fnet_w, pool_w, pool_scale, conv_w, conv_b, conv_ln_g, conv_ln_b, conv_w_out, w_branch, w_out, g_norm2, mlp_w1, mlp_w2, g_final):
    bsz, seq, d = x.shape
    ctx_len = ctx.shape[1]
    depth = w_mod.shape[0]
    wb = W_BRANCH
    assert bsz < 8 and seq % 512 == 0 and ctx_len % (S5_CHUNK * 8) == 0

    tn = 512
    pos_tabs = _pos_tables(seq // GRID_W, tn)
    cpad = jnp.zeros((8, d), F32).at[:bsz].set(c).at[bsz].set(c_ctx)
    mod_all = _mod_call(cpad, w_mod, b_mod)
    ctx_row = bsz
    e_all, w2_all, are_all, aim_all = _s5_prep_call(s5_lam_re, s5_lam_im, s5_log_dt, s5_b_re, s5_b_im,
                                                    s5_c_re, s5_c_im, s5_d)
    vec = lambda a: a.reshape(depth, 1, -1).astype(F32)
    g1_all, g2_all = vec(g_norm1), vec(g_norm2)
    w5_all, *big = _cast_call(w_in, mlp_w1, mlp_w2, w_branch, w_out, 5 * wb)
    wgate, w1b, w2b, wbrb, woutb = big
    big = (wgate, wbrb, woutb, w1b, w2b)
    pool_bd = jnp.zeros((depth, wb, wb), F32)
    pw_n = pool_w.shape[2]
    for k in range(pool_w.shape[1]):
        pool_bd = pool_bd.at[:, k * pw_n:(k + 1) * pw_n, k * pw_n:(k + 1) * pw_n].set(pool_w[:, k])
    cw_all = jnp.zeros((depth, 32, wb), F32).at[:, :CONV_WIDTH].set(conv_w)
    small = (s5_w_glu.astype(BF16), fnet_w.astype(BF16), pool_bd.astype(BF16), vec(pool_scale), cw_all,
             vec(conv_b), vec(conv_ln_g), vec(conv_ln_b), conv_w_out.astype(BF16))
    gf = g_final.reshape(1, -1).astype(F32)
    wcs = lambda n: jnp.asarray(_fnet_consts(n // FFT_RADIX)[3])

    xc = ctx
    for l in range(depth):
        last = l == depth - 1
        us_c, uf_c, up_c, v_c = _pre_call(xc, None, mod_all, g1_all, w5_all, layer=l, mod_row=ctx_row,
                                          tn=ctx_len)
        us, uf, up, v = _pre_call(x, pos_tabs if l == 0 else None, mod_all, g1_all, w5_all, layer=l,
                                  mod_row=None, tn=tn)

        ys_c, ys = _s5_call(us_c, us, e_all, w2_all, are_all, aim_all, layer=l)

        z = _fnet_call(uf)
        x = _merge_call(x, pos_tabs if l == 0 else None, mod_all, g1_all, g2_all, gf if last else None,
                        ys, z, up, v, wcs(seq), small, big, layer=l, mod_row=None, tn=tn)
        if not last:
            z_c = _fnet_call(uf_c)
            xc = _merge_call(xc, None, mod_all, g1_all, g2_all, None, ys_c, z_c, up_c, v_c, wcs(ctx_len),
                             small, big, layer=l, mod_row=ctx_row, tn=ctx_len)
    return x
```
